```python
import math
import jax, jax.numpy as jnp
from jax import lax
import numpy as np

D_MODEL = 4096
BATCH = 2
SEQ = 4096
DEPTH = 1

PLE_DIM = 256
RMS_EPS = 1e-6
DA_HEADS = 8
DA_HEAD_DIM = 128
DA_WIDTH = DA_HEADS * 2 * DA_HEAD_DIM
Q_BLOCK = 128
GLA_HEADS = 4
GLA_DK = 256
GLA_DV = 512
GLA_KEY_WIDTH = GLA_HEADS * GLA_DK
GLA_WIDTH = GLA_HEADS * GLA_DV
GLA_GATE_RANK = 16
GLA_TAU = 16.0
GLA_CHUNK = 64
MIX_WIDTH = DA_WIDTH + GLA_WIDTH
N_IN = 3 * DA_WIDTH + 2 * GLA_KEY_WIDTH + 2 * GLA_WIDTH + GLA_GATE_RANK
N_EXPERTS = 32
TOP_K = 4
D_EXPERT = 1536
SWIGLU_LIMIT = 7.0
SWIGLU_ALPHA = 1.702
MOE_BLOCK = 128

kernel_name = 'hybrid_diffattn_gla_moe_ple'


def rms_norm(x, g):
    xf = x.astype(jnp.float32)
    y = xf * lax.rsqrt(jnp.mean(xf * xf, axis=-1, keepdims=True) + RMS_EPS)
    return (y * g.astype(jnp.float32)).astype(x.dtype)


def alibi_slopes(n_heads):
    return jnp.exp2(-8.0 * jnp.arange(1, n_heads + 1, dtype=jnp.float32) / n_heads)


def split_in(proj):
    sizes = (DA_WIDTH, DA_WIDTH, DA_WIDTH, GLA_KEY_WIDTH, GLA_KEY_WIDTH, GLA_WIDTH, GLA_WIDTH, GLA_GATE_RANK)
    points = np.cumsum(sizes)[:-1].tolist()
    return jnp.split(proj, points, axis=-1)


def diff_attention(q, k, v, lam, subln_g, lam_init):
    B, S, H, _, d = q.shape
    q = q.astype(jnp.float32) * (d ** -0.5)
    k = k.astype(jnp.float32)
    v = v.astype(jnp.float32)
    nq = S // Q_BLOCK
    slopes = alibi_slopes(H)
    kpos = jnp.arange(S)
    qb = q.reshape(B, nq, Q_BLOCK, H, 2, d).transpose(1, 0, 2, 3, 4, 5)

    def block(args):
        qi, i = args
        qpos = i * Q_BLOCK + jnp.arange(Q_BLOCK)
        dist = qpos[:, None] - kpos[None, :]
        s = jnp.einsum('bqhmd,bkhmd->bhmqk', qi, k)
        s = s - (slopes[:, None, None] * jnp.abs(dist).astype(jnp.float32))[None, :, None]
        s = jnp.where((dist >= 0)[None, None, None], s, -jnp.inf)
        a = jax.nn.softmax(s, axis=-1)
        w = a[:, :, 0] - lam * a[:, :, 1]
        return jnp.einsum('bhqk,bkhe->bqhe', w, v)

    o = lax.map(block, (qb, jnp.arange(nq)))
    o = o.transpose(1, 0, 2, 3, 4).reshape(B, S, H, 2 * d)
    o = rms_norm(o, subln_g) * (1.0 - lam_init)
    return o.reshape(B, S, H * 2 * d)


def gla_chunked(q, k, v, log_a):
    B, S, H, dk = q.shape
    dv = v.shape[-1]
    C = GLA_CHUNK
    n = S // C

    def to_chunks(t):
        return t.reshape(B, n, C, H, t.shape[-1]).transpose(1, 0, 3, 2, 4)

    q, k, v, log_a = to_chunks(q), to_chunks(k), to_chunks(v), to_chunks(log_a)
    b = jnp.cumsum(log_a, axis=-2)
    b_last = b[..., -1:, :]
    b_mid = b[..., C // 2:C // 2 + 1, :]
    qg = q * jnp.exp(b - b_mid)
    kg = k * jnp.exp(b_mid - b)
    att = jnp.einsum('nbhcd,nbhsd->nbhcs', qg, kg)
    att = jnp.where(jnp.tril(jnp.ones((C, C), dtype=bool)), att, 0.0)
    o_intra = jnp.einsum('nbhcs,nbhse->nbhce', att, v)
    q_inter = q * jnp.exp(b)
    k_state = k * jnp.exp(b_last - b)
    decay = jnp.exp(b_last[..., 0, :])

    def step(state, inp):
        qi, ki, vi, di = inp
        o = jnp.einsum('bhcd,bhde->bhce', qi, state)
        state = di[..., None] * state + jnp.einsum('bhcd,bhce->bhde', ki, vi)
        return state, o

    state0 = jnp.zeros((B, H, dk, dv), jnp.float32)
    _, o_inter = lax.scan(step, state0, (q_inter, k_state, v, decay))
    o = (o_intra + o_inter).transpose(1, 0, 3, 2, 4)
    return o.reshape(B, S, H, dv)


def moe(hn, w_router, b_router, w_gate_up, b_gate_up, w_down, b_down):
    B, S, D = hn.shape
    T = B * S
    xt = hn.reshape(T, D)
    logits = (xt @ w_router + b_router).astype(jnp.float32)
    top_vals, top_idx = lax.top_k(logits, TOP_K)
    top_w = jax.nn.softmax(top_vals, axis=-1)
    n_slots = T * TOP_K
    slot_expert = top_idx.reshape(-1)
    slot_token = jnp.arange(n_slots, dtype=jnp.int32) // TOP_K
    slot_weight = top_w.reshape(-1)
    order = jnp.argsort(slot_expert)
    sorted_expert = slot_expert[order]
    counts = jnp.bincount(slot_expert, length=N_EXPERTS)
    padded = (counts + MOE_BLOCK - 1) // MOE_BLOCK * MOE_BLOCK
    group_start = jnp.cumsum(counts) - counts
    padded_end = jnp.cumsum(padded)
    padded_start = padded_end - padded
    rank = jnp.arange(n_slots, dtype=jnp.int32) - group_start[sorted_expert]
    dest = padded_start[sorted_expert] + rank
    n_blocks = -(-n_slots // MOE_BLOCK) + N_EXPERTS
    n_rows = n_blocks * MOE_BLOCK
    row_token = jnp.zeros((n_rows,), jnp.int32).at[dest].set(slot_token[order])
    row_weight = jnp.zeros((n_rows,), jnp.float32).at[dest].set(slot_weight[order])
    block_start = jnp.arange(n_blocks, dtype=jnp.int32) * MOE_BLOCK
    block_expert = jnp.minimum(jnp.searchsorted(padded_end, block_start, side='right'), N_EXPERTS - 1)

    def step(acc, inp):
        tok, wt, e = inp
        xe = xt[tok]
        gu = xe @ w_gate_up[e] + b_gate_up[e]
        gate = jnp.minimum(gu[:, 0::2], SWIGLU_LIMIT)
        up = jnp.clip(gu[:, 1::2], -SWIGLU_LIMIT, SWIGLU_LIMIT)
        act = gate * jax.nn.sigmoid(gate * SWIGLU_ALPHA) * (up + 1.0)
        ye = act @ w_down[e] + b_down[e]
        acc = acc.at[tok].add(ye.astype(jnp.float32) * wt[:, None])
        return acc, None

    acc0 = jnp.zeros((T, D), jnp.float32)
    acc, _ = lax.scan(step, acc0, (row_token.reshape(n_blocks, MOE_BLOCK),
                                   row_weight.reshape(n_blocks, MOE_BLOCK), block_expert))
    return acc.reshape(B, S, D).astype(hn.dtype)


def setup_inputs(seed: int = 0) -> dict:
    key = jax.random.key(seed)
    ks = jax.random.split(key, 32)
    f32 = jnp.float32
    L = DEPTH

    def nrm(k, shape, scale):
        return jax.random.normal(k, shape, f32) * scale

    def gain(k, shape):
        return 1.0 + 0.02 * jax.random.normal(k, shape, f32)

    return {
        'x': nrm(ks[0], (BATCH, SEQ, D_MODEL), 1.0),
        'p': nrm(ks[1], (DEPTH, BATCH, SEQ, PLE_DIM), 1.0),
        'norm_mix': gain(ks[2], (L, D_MODEL)),
        'w_in': nrm(ks[3], (L, D_MODEL, N_IN), D_MODEL ** -0.5),
        'w_alpha_up': nrm(ks[4], (L, GLA_GATE_RANK, GLA_KEY_WIDTH), GLA_GATE_RANK ** -0.5),
        'b_alpha': nrm(ks[5], (L, GLA_KEY_WIDTH), 0.1),
        'lambda_q1': nrm(ks[6], (L, DA_HEAD_DIM), 0.1),
        'lambda_k1': nrm(ks[7], (L, DA_HEAD_DIM), 0.1),
        'lambda_q2': nrm(ks[8], (L, DA_HEAD_DIM), 0.1),
        'lambda_k2': nrm(ks[9], (L, DA_HEAD_DIM), 0.1),
        'g_diff_subln': gain(ks[10], (L, 2 * DA_HEAD_DIM)),
        'g_gla_out': gain(ks[11], (L, GLA_DV)),
        'w_branch': nrm(ks[12], (L, MIX_WIDTH, D_MODEL), DA_WIDTH ** -0.5),
        'w_merge_gate': nrm(ks[13], (L, D_MODEL, 2 * D_MODEL), D_MODEL ** -0.5),
        'b_merge_gate': nrm(ks[14], (L, 2 * D_MODEL), 0.02),
        'w_out': nrm(ks[15], (L, D_MODEL, D_MODEL), D_MODEL ** -0.5),
        'norm_moe': gain(ks[16], (L, D_MODEL)),
        'w_router': nrm(ks[17], (L, D_MODEL, N_EXPERTS), D_MODEL ** -0.5),
        'b_router': nrm(ks[18], (L, N_EXPERTS), 0.01),
        'w_gate_up': nrm(ks[19], (L, N_EXPERTS, D_MODEL, 2 * D_EXPERT), D_MODEL ** -0.5),
        'b_gate_up': nrm(ks[20], (L, N_EXPERTS, 2 * D_EXPERT), 0.02),
        'w_down': nrm(ks[21], (L, N_EXPERTS, D_EXPERT, D_MODEL), D_EXPERT ** -0.5),
        'b_down': nrm(ks[22], (L, N_EXPERTS, D_MODEL), 0.02),
        'norm_ple': gain(ks[23], (L, D_MODEL)),
        'w_ple_proj': nrm(ks[24], (L, PLE_DIM, D_MODEL), PLE_DIM ** -0.5),
        'w_ple_gate': nrm(ks[25], (L, D_MODEL, D_MODEL), D_MODEL ** -0.5),
        'norm_final': gain(ks[26], (D_MODEL,)),
    }


def reference(x, p, norm_mix, w_in, w_alpha_up, b_alpha, lambda_q1, lambda_k1, lambda_q2, lambda_k2,
              g_diff_subln, g_gla_out, w_branch, w_merge_gate, b_merge_gate, w_out, norm_moe,
              w_router, b_router, w_gate_up, b_gate_up, w_down, b_down, norm_ple, w_ple_proj,
              w_ple_gate, norm_final):
    B, S, D = x.shape
    h = x
    for i in range(DEPTH):
        xn = rms_norm(h, norm_mix[i])
        proj = xn @ w_in[i]
        qa, ka, va, qg, kg, vg, og, ag = split_in(proj)
        lam_init = 0.8 - 0.6 * math.exp(-0.3 * i)
        lam = (jnp.exp(jnp.sum(lambda_q1[i].astype(jnp.float32) * lambda_k1[i].astype(jnp.float32)))
               - jnp.exp(jnp.sum(lambda_q2[i].astype(jnp.float32) * lambda_k2[i].astype(jnp.float32)))
               + lam_init)
        o_a = diff_attention(qa.reshape(B, S, DA_HEADS, 2, DA_HEAD_DIM),
                             ka.reshape(B, S, DA_HEADS, 2, DA_HEAD_DIM),
                             va.reshape(B, S, DA_HEADS, 2 * DA_HEAD_DIM),
                             lam, g_diff_subln[i], lam_init)
        log_a = jax.nn.log_sigmoid((ag @ w_alpha_up[i] + b_alpha[i]).astype(jnp.float32)) / GLA_TAU
        o_b = gla_chunked(qg.astype(jnp.float32).reshape(B, S, GLA_HEADS, GLA_DK) * (GLA_DK ** -0.5),
                          kg.astype(jnp.float32).reshape(B, S, GLA_HEADS, GLA_DK),
                          vg.astype(jnp.float32).reshape(B, S, GLA_HEADS, GLA_DV),
                          log_a.reshape(B, S, GLA_HEADS, GLA_DK))
        o_b = rms_norm(o_b, g_gla_out[i]) * jax.nn.silu(og.astype(jnp.float32).reshape(B, S, GLA_HEADS, GLA_DV))
        o_b = o_b.reshape(B, S, GLA_WIDTH)
        y_a = o_a.astype(h.dtype) @ w_branch[i, :DA_WIDTH]
        y_b = o_b.astype(h.dtype) @ w_branch[i, DA_WIDTH:]
        gates = jax.nn.sigmoid(xn @ w_merge_gate[i] + b_merge_gate[i])
        g_a, g_b = jnp.split(gates, 2, axis=-1)
        h = h + (g_a * y_a + g_b * y_b) @ w_out[i]
        h = h + moe(rms_norm(h, norm_moe[i]), w_router[i], b_router[i], w_gate_up[i], b_gate_up[i],
                    w_down[i], b_down[i])
        ple_gate = jax.nn.sigmoid(rms_norm(h, norm_ple[i]) @ w_ple_gate[i])
        h = h + ple_gate * (p[i] @ w_ple_proj[i])
    return rms_norm(h, norm_final)
```

```python
import functools
import math

import jax
import jax.numpy as jnp
from jax import lax
from jax.experimental import pallas as pl
from jax.experimental.pallas import tpu as pltpu

F32 = jnp.float32
BF16 = jnp.bfloat16

RMS_EPS = 1e-6
DA_HEADS = 8
DA_HEAD_DIM = 128
DA_WIDTH = DA_HEADS * 2 * DA_HEAD_DIM
GLA_HEADS = 4
GLA_DK = 256
GLA_DV = 512
GLA_KEY_WIDTH = GLA_HEADS * GLA_DK
GLA_WIDTH = GLA_HEADS * GLA_DV
GLA_GATE_RANK = 16
GLA_TAU = 16.0
GLA_CHUNK = 64
N_PROJ = 3 * DA_WIDTH + 2 * GLA_KEY_WIDTH + 2 * GLA_WIDTH
N_EXPERTS = 32
TOP_K = 4
SWIGLU_LIMIT = 7.0
SWIGLU_ALPHA = 1.702
LANES = 128
MOE_ROWS = 256

VMEM_LIMIT = 56 * 1024 * 1024


def _params(sem, vmem=VMEM_LIMIT):
    return pltpu.CompilerParams(dimension_semantics=sem, vmem_limit_bytes=vmem)


def _rmsnorm_kernel(x_ref, g_ref, o_ref):
    x = x_ref[...]
    ms = jnp.mean(x * x, axis=-1, keepdims=True)
    o_ref[...] = (x * lax.rsqrt(ms + RMS_EPS) * g_ref[...]).astype(o_ref.dtype)


def rmsnorm(x, g, out_dtype, tm=256):
    t, d = x.shape
    return pl.pallas_call(
        _rmsnorm_kernel,
        out_shape=jax.ShapeDtypeStruct((t, d), out_dtype),
        grid=(t // tm,),
        in_specs=[pl.BlockSpec((tm, d), lambda i: (i, 0)),
                  pl.BlockSpec((1, d), lambda i: (0, 0))],
        out_specs=pl.BlockSpec((tm, d), lambda i: (i, 0)),
        compiler_params=_params(("parallel",)),
        name="rmsnorm",
    )(x, g.reshape(1, d))


def _mm_kernel(x_ref, w_ref, o_ref):
    o_ref[...] = jnp.dot(x_ref[...], w_ref[...], preferred_element_type=F32).astype(o_ref.dtype)


def matmul(x, w, out_dtype, tm=1024, tn=1024):
    m, k = x.shape
    n = w.shape[1]
    tm, tn = min(tm, m), min(tn, n)
    return pl.pallas_call(
        _mm_kernel,
        out_shape=jax.ShapeDtypeStruct((m, n), out_dtype),
        grid=(m // tm, n // tn),
        in_specs=[pl.BlockSpec((tm, k), lambda i, j: (i, 0)),
                  pl.BlockSpec((k, tn), lambda i, j: (0, j))],
        out_specs=pl.BlockSpec((tm, tn), lambda i, j: (i, j)),
        compiler_params=_params(("parallel", "parallel")),
        name="matmul",
    )(x, w)


def _mm_bias_sigmoid_kernel(x_ref, w_ref, b_ref, o_ref):
    z = jnp.dot(x_ref[...], w_ref[...], preferred_element_type=F32) + b_ref[...]
    o_ref[...] = jax.nn.sigmoid(z).astype(o_ref.dtype)


def matmul_bias_sigmoid(x, w, b, out_dtype, tm=1024, tn=1024):
    m, k = x.shape
    n = w.shape[1]
    return pl.pallas_call(
        _mm_bias_sigmoid_kernel,
        out_shape=jax.ShapeDtypeStruct((m, n), out_dtype),
        grid=(m // tm, n // tn),
        in_specs=[pl.BlockSpec((tm, k), lambda i, j: (i, 0)),
                  pl.BlockSpec((k, tn), lambda i, j: (0, j)),
                  pl.BlockSpec((1, tn), lambda i, j: (0, j))],
        out_specs=pl.BlockSpec((tm, tn), lambda i, j: (i, j)),
        compiler_params=_params(("parallel", "parallel")),
        name="matmul_bias_sigmoid",
    )(x, w, b.reshape(1, n))


def _mm_residual_kernel(x_ref, w_ref, r_ref, o_ref):
    o_ref[...] = r_ref[...] + jnp.dot(x_ref[...], w_ref[...], preferred_element_type=F32)


def matmul_residual(x, w, r, tm=512, tn=1024):
    m, k = x.shape
    n = w.shape[1]
    return pl.pallas_call(
        _mm_residual_kernel,
        out_shape=jax.ShapeDtypeStruct((m, n), F32),
        grid=(m // tm, n // tn),
        in_specs=[pl.BlockSpec((tm, k), lambda i, j: (i, 0)),
                  pl.BlockSpec((k, tn), lambda i, j: (0, j)),
                  pl.BlockSpec((tm, tn), lambda i, j: (i, j))],
        out_specs=pl.BlockSpec((tm, tn), lambda i, j: (i, j)),
        compiler_params=_params(("parallel", "parallel")),
        name="matmul_residual",
    )(x, w, r)


def _merge_kernel(oa_ref, ob_ref, wa_ref, wb_ref, ga_ref, gb_ref, o_ref):
    ya = jnp.dot(oa_ref[...], wa_ref[...], preferred_element_type=F32)
    yb = jnp.dot(ob_ref[...], wb_ref[...], preferred_element_type=F32)
    o_ref[...] = (ga_ref[...].astype(F32) * ya + gb_ref[...].astype(F32) * yb).astype(o_ref.dtype)


def merge_branches(o_a, o_b, w_branch, gates, tm=1024, tn=1024):
    m, ka = o_a.shape
    kb = o_b.shape[1]
    n = w_branch.shape[1]
    nb = n // tn
    return pl.pallas_call(
        _merge_kernel,
        out_shape=jax.ShapeDtypeStruct((m, n), BF16),
        grid=(m // tm, nb),
        in_specs=[pl.BlockSpec((tm, ka), lambda i, j: (i, 0)),
                  pl.BlockSpec((tm, kb), lambda i, j: (i, 0)),
                  pl.BlockSpec((ka, tn), lambda i, j: (0, j)),
                  pl.BlockSpec((kb, tn), lambda i, j: (1, j)),
                  pl.BlockSpec((tm, tn), lambda i, j: (i, j)),
                  pl.BlockSpec((tm, tn), lambda i, j, nb=nb: (i, j + nb))],
        out_specs=pl.BlockSpec((tm, tn), lambda i, j: (i, j)),
        compiler_params=_params(("parallel", "parallel")),
        name="merge_branches",
    )(o_a, o_b, w_branch, w_branch, gates, gates)


def _ple_kernel(hn_ref, wg_ref, p_ref, wp_ref, h_ref, o_ref):
    g = jax.nn.sigmoid(jnp.dot(hn_ref[...], wg_ref[...], preferred_element_type=F32))
    pp = jnp.dot(p_ref[...], wp_ref[...], preferred_element_type=F32)
    o_ref[...] = h_ref[...] + g * pp


def ple_update(hn, w_gate, p, w_proj, h, tm=512, tn=1024):
    m, k = hn.shape
    kp = p.shape[1]
    n = w_gate.shape[1]
    return pl.pallas_call(
        _ple_kernel,
        out_shape=jax.ShapeDtypeStruct((m, n), F32),
        grid=(m // tm, n // tn),
        in_specs=[pl.BlockSpec((tm, k), lambda i, j: (i, 0)),
                  pl.BlockSpec((k, tn), lambda i, j: (0, j)),
                  pl.BlockSpec((tm, kp), lambda i, j: (i, 0)),
                  pl.BlockSpec((kp, tn), lambda i, j: (0, j)),
                  pl.BlockSpec((tm, tn), lambda i, j: (i, j))],
        out_specs=pl.BlockSpec((tm, tn), lambda i, j: (i, j)),
        compiler_params=_params(("parallel", "parallel")),
        name="ple_update",
    )(hn, w_gate, p, w_proj, h)


ATT_BLOCK = 512


def _diff_attn_kernel(q_ref, k_ref, v_ref, lq1_ref, lk1_ref, lq2_ref, lk2_ref, g_ref, o_ref,
                      m_scr, l_scr, acc_scr, *, lam_init):
    tq = q_ref.shape[0]
    d = DA_HEAD_DIM
    h = pl.program_id(1)
    i = pl.program_id(2)
    slope = jnp.exp2((-8.0 / DA_HEADS) * (h + 1).astype(F32) * jnp.ones((1, 1), F32))
    scale = d ** -0.5

    q = (q_ref[...].astype(F32) * scale).astype(BF16)
    qs = (q[:, :d], q[:, d:])
    row = lax.broadcasted_iota(jnp.int32, (tq, tq), 0)
    col = lax.broadcasted_iota(jnp.int32, (tq, tq), 1)
    rel = (row - col).astype(F32) * slope

    m_scr[...] = jnp.full(m_scr.shape, -jnp.inf, F32)
    l_scr[...] = jnp.zeros(l_scr.shape, F32)
    acc_scr[...] = jnp.zeros(acc_scr.shape, F32)

    def block(j, masked):
        start = pl.multiple_of(j * tq, tq)
        kj = k_ref[pl.ds(start, tq), :]
        vj = v_ref[pl.ds(start, tq), :]
        off = slope * ((i - j) * tq).astype(F32)
        for mp in range(2):
            s = lax.dot_general(qs[mp], kj[:, mp * d:(mp + 1) * d], (((1,), (1,)), ((), ())),
                                preferred_element_type=F32)
            s = s - (rel + off)
            if masked:
                s = jnp.where(row >= col, s, -jnp.inf)
            m_old = m_scr[mp]
            m_new = jnp.maximum(m_old, jnp.max(s, axis=-1, keepdims=True))
            p = jnp.exp(s - m_new)
            alpha = jnp.exp(m_old - m_new)
            l_scr[mp] = alpha * l_scr[mp] + jnp.sum(p, axis=-1, keepdims=True)
            acc_scr[mp] = alpha * acc_scr[mp] + jnp.dot(p.astype(BF16), vj, preferred_element_type=F32)
            m_scr[mp] = m_new

    def body(j, carry):
        block(j, False)
        return carry

    lax.fori_loop(0, i, body, 0)
    block(i, True)

    lam = (jnp.exp(jnp.sum(lq1_ref[...] * lk1_ref[...], axis=-1, keepdims=True))
           - jnp.exp(jnp.sum(lq2_ref[...] * lk2_ref[...], axis=-1, keepdims=True)) + lam_init)
    o = acc_scr[0] / l_scr[0] - lam * (acc_scr[1] / l_scr[1])
    ms = jnp.mean(o * o, axis=-1, keepdims=True)
    o = o * lax.rsqrt(ms + RMS_EPS) * g_ref[...] * (1.0 - lam_init)
    o_ref[...] = o.astype(o_ref.dtype)


def diff_attention(proj, batch, seq, lq1, lk1, lq2, lk2, g_subln, lam_init):
    tq = min(ATT_BLOCK, seq)
    nq = seq // tq
    hw = 2 * DA_HEAD_DIM
    kcol = DA_WIDTH // hw
    vcol = 2 * DA_WIDTH // hw
    vec = lambda a: a.reshape(1, -1).astype(F32)
    const = lambda b, h, i: (0, 0)
    return pl.pallas_call(
        functools.partial(_diff_attn_kernel, lam_init=lam_init),
        out_shape=jax.ShapeDtypeStruct((batch * seq, DA_WIDTH), BF16),
        grid=(batch, DA_HEADS, nq),
        in_specs=[pl.BlockSpec((tq, hw), lambda b, h, i: (b * nq + i, h)),
                  pl.BlockSpec((seq, hw), lambda b, h, i: (b, kcol + h)),
                  pl.BlockSpec((seq, hw), lambda b, h, i: (b, vcol + h)),
                  pl.BlockSpec((1, DA_HEAD_DIM), const),
                  pl.BlockSpec((1, DA_HEAD_DIM), const),
                  pl.BlockSpec((1, DA_HEAD_DIM), const),
                  pl.BlockSpec((1, DA_HEAD_DIM), const),
                  pl.BlockSpec((1, hw), const)],
        out_specs=pl.BlockSpec((tq, hw), lambda b, h, i: (b * nq + i, h)),
        scratch_shapes=[pltpu.VMEM((2, tq, 1), F32),
                        pltpu.VMEM((2, tq, 1), F32),
                        pltpu.VMEM((2, tq, hw), F32)],
        compiler_params=_params(("parallel", "parallel", "arbitrary")),
        name="diff_attention",
    )(proj, proj, proj, vec(lq1), vec(lk1), vec(lq2), vec(lk2), vec(g_subln))


GLA_STEP = 512


def _gla_kernel(q_ref, k_ref, v_ref, og_ref, ag_ref, wa_ref, ba_ref, g_ref, o_ref, state_scr):
    c = GLA_CHUNK
    n_chunks = q_ref.shape[0] // c

    @pl.when(pl.program_id(2) == 0)
    def _():
        state_scr[...] = jnp.zeros(state_scr.shape, F32)

    row = lax.broadcasted_iota(jnp.int32, (c, c), 0)
    col = lax.broadcasted_iota(jnp.int32, (c, c), 1)
    tril = row >= col
    tri_ones = tril.astype(F32)
    hi = lax.Precision.HIGHEST

    for n in range(n_chunks):
        sl = pl.ds(n * c, c)
        z = jnp.dot(ag_ref[sl, :], wa_ref[...], precision=hi, preferred_element_type=F32) + ba_ref[...]
        log_a = jax.nn.log_sigmoid(z) * (1.0 / GLA_TAU)
        b = jnp.dot(tri_ones, log_a, precision=hi, preferred_element_type=F32)
        b_mid = b[c // 2:c // 2 + 1, :]
        b_last = b[c - 1:c, :]
        q = q_ref[sl, :].astype(F32) * (GLA_DK ** -0.5)
        k = k_ref[sl, :].astype(F32)
        v = v_ref[sl, :]
        qg = (q * jnp.exp(b - b_mid)).astype(BF16)
        kg = (k * jnp.exp(b_mid - b)).astype(BF16)
        att = lax.dot_general(qg, kg, (((1,), (1,)), ((), ())), preferred_element_type=F32)
        att = jnp.where(tril, att, 0.0).astype(BF16)
        o = jnp.dot(att, v, preferred_element_type=F32)
        q_inter = (q * jnp.exp(b)).astype(BF16)
        state = state_scr[...]
        o = o + lax.dot_general(q_inter, state.astype(BF16), (((1,), (1,)), ((), ())),
                                preferred_element_type=F32)
        k_state = (k * jnp.exp(b_last - b)).astype(BF16)
        kv = lax.dot_general(v, k_state, (((0,), (0,)), ((), ())), preferred_element_type=F32)
        state_scr[...] = state * jnp.exp(b_last) + kv
        ms = jnp.mean(o * o, axis=-1, keepdims=True)
        og = og_ref[sl, :].astype(F32)
        o = o * lax.rsqrt(ms + RMS_EPS) * g_ref[...] * (og * jax.nn.sigmoid(og))
        o_ref[sl, :] = o.astype(o_ref.dtype)


def gla(proj, ag, w_alpha_up, b_alpha, g_out, batch, seq):
    ts = min(GLA_STEP, seq)
    ns = seq // ts
    qcol = 3 * DA_WIDTH // GLA_DK
    kcol = qcol + GLA_KEY_WIDTH // GLA_DK
    vcol = (3 * DA_WIDTH + 2 * GLA_KEY_WIDTH) // GLA_DV
    ocol = vcol + GLA_WIDTH // GLA_DV
    rank = ag.shape[1]
    return pl.pallas_call(
        _gla_kernel,
        out_shape=jax.ShapeDtypeStruct((batch * seq, GLA_WIDTH), BF16),
        grid=(batch, GLA_HEADS, ns),
        in_specs=[pl.BlockSpec((ts, GLA_DK), lambda b, h, i: (b * ns + i, qcol + h)),
                  pl.BlockSpec((ts, GLA_DK), lambda b, h, i: (b * ns + i, kcol + h)),
                  pl.BlockSpec((ts, GLA_DV), lambda b, h, i: (b * ns + i, vcol + h)),
                  pl.BlockSpec((ts, GLA_DV), lambda b, h, i: (b * ns + i, ocol + h)),
                  pl.BlockSpec((ts, rank), lambda b, h, i: (b * ns + i, 0)),
                  pl.BlockSpec((rank, GLA_DK), lambda b, h, i: (0, h)),
                  pl.BlockSpec((1, GLA_DK), lambda b, h, i: (0, h)),
                  pl.BlockSpec((1, GLA_DV), lambda b, h, i: (0, 0))],
        out_specs=pl.BlockSpec((ts, GLA_DV), lambda b, h, i: (b * ns + i, h)),
        scratch_shapes=[pltpu.VMEM((GLA_DV, GLA_DK), F32)],
        compiler_params=_params(("parallel", "parallel", "arbitrary")),
        name="gla",
    )(proj, proj, proj, proj, ag, w_alpha_up, b_alpha.reshape(1, -1), g_out.reshape(1, -1))


def _router_kernel(h_ref, g_ref, w_ref, b_ref, hn_ref, idx_ref, wt_ref, rank_ref, cnt_ref, cnt_scr):
    tm = h_ref.shape[0]

    @pl.when(pl.program_id(0) == 0)
    def _():
        cnt_scr[...] = jnp.zeros(cnt_scr.shape, F32)

    x = h_ref[...]
    ms = jnp.mean(x * x, axis=-1, keepdims=True)
    hn = x * lax.rsqrt(ms + RMS_EPS) * g_ref[...]
    hn_ref[...] = hn
    logits = jnp.dot(hn, w_ref[...], precision=lax.Precision.HIGHEST, preferred_element_type=F32) + b_ref[...]
    lane_i = lax.broadcasted_iota(jnp.int32, (tm, LANES), 1)
    lane = lane_i.astype(F32)
    logits = jnp.where(lane_i < N_EXPERTS, logits, -jnp.inf)

    row = lax.broadcasted_iota(jnp.int32, (tm, tm), 0)
    col = lax.broadcasted_iota(jnp.int32, (tm, tm), 1)
    strict = (row > col).astype(BF16)

    work = logits
    sel = jnp.zeros((tm, LANES), jnp.bool_)
    idx_out = jnp.zeros((tm, LANES), F32)
    val_out = jnp.full((tm, LANES), -jnp.inf, F32)
    onehots = []
    for kk in range(TOP_K):
        mx = jnp.max(work, axis=-1, keepdims=True)
        idx = jnp.min(jnp.where(work == mx, lane, float(LANES)), axis=-1, keepdims=True)
        onehot = lane == idx
        onehots.append(onehot)
        sel = jnp.logical_or(sel, onehot)
        idx_out = jnp.where(lane_i == kk, idx, idx_out)
        val_out = jnp.where(lane_i == kk, mx, val_out)
        work = jnp.where(onehot, -jnp.inf, work)

    e = jnp.exp(val_out - jnp.max(val_out, axis=-1, keepdims=True))
    wt_ref[...] = e / jnp.sum(e, axis=-1, keepdims=True)
    idx_ref[...] = idx_out.astype(jnp.int32)

    self = sel.astype(F32)
    rank_dense = jnp.dot(strict, self.astype(BF16), preferred_element_type=F32) + cnt_scr[...]
    rank_out = jnp.zeros((tm, LANES), F32)
    for kk in range(TOP_K):
        r = jnp.sum(jnp.where(onehots[kk], rank_dense, 0.0), axis=-1, keepdims=True)
        rank_out = jnp.where(lane_i == kk, r, rank_out)
    rank_ref[...] = rank_out.astype(jnp.int32)
    cnt_scr[...] = cnt_scr[...] + jnp.sum(self, axis=0, keepdims=True)
    cnt_ref[...] = cnt_scr[...].astype(jnp.int32)


def router(h, g, w_router, b_router, tm=256):
    t, d = h.shape
    wpad = jnp.zeros((d, LANES), F32).at[:, :N_EXPERTS].set(w_router)
    bpad = jnp.zeros((1, LANES), F32).at[0, :N_EXPERTS].set(b_router)
    tile = lambda i: (i, 0)
    const = lambda i: (0, 0)
    return pl.pallas_call(
        _router_kernel,
        out_shape=[jax.ShapeDtypeStruct((t, d), F32),
                   jax.ShapeDtypeStruct((t, LANES), jnp.int32),
                   jax.ShapeDtypeStruct((t, LANES), F32),
                   jax.ShapeDtypeStruct((t, LANES), jnp.int32),
                   jax.ShapeDtypeStruct((1, LANES), jnp.int32)],
        grid=(t // tm,),
        in_specs=[pl.BlockSpec((tm, d), tile), pl.BlockSpec((1, d), const),
                  pl.BlockSpec((d, LANES), const), pl.BlockSpec((1, LANES), const)],
        out_specs=[pl.BlockSpec((tm, d), tile), pl.BlockSpec((tm, LANES), tile),
                   pl.BlockSpec((tm, LANES), tile), pl.BlockSpec((tm, LANES), tile),
                   pl.BlockSpec((1, LANES), const)],
        scratch_shapes=[pltpu.VMEM((1, LANES), F32)],
        compiler_params=_params(("arbitrary",)),
        name="router",
    )(h, g.reshape(1, d), wpad, bpad)


def _dispatch_kernel(tok_ref, x_hbm, o_ref, buf, sem):
    tm = o_ref.shape[0]

    def copy(r):
        return pltpu.make_async_copy(x_hbm.at[pl.ds(tok_ref[0, 0, r], 1), :], buf.at[pl.ds(r, 1), :], sem)

    def start(r, carry):
        copy(r).start()
        return carry

    def wait(r, carry):
        copy(r).wait()
        return carry

    lax.fori_loop(0, tm, start, 0)
    lax.fori_loop(0, tm, wait, 0)
    o_ref[...] = buf[...].astype(o_ref.dtype)


def dispatch(x, row_token, tm=MOE_ROWS):
    t, d = x.shape
    n_rows = row_token.shape[0]
    return pl.pallas_call(
        _dispatch_kernel,
        out_shape=jax.ShapeDtypeStruct((n_rows, d), BF16),
        grid=(n_rows // tm,),
        in_specs=[pl.BlockSpec((1, 1, tm), lambda i: (i, 0, 0), memory_space=pltpu.SMEM),
                  pl.BlockSpec(memory_space=pl.ANY)],
        out_specs=pl.BlockSpec((tm, d), lambda i: (i, 0)),
        scratch_shapes=[pltpu.VMEM((tm, d), F32), pltpu.SemaphoreType.DMA(())],
        compiler_params=_params(("arbitrary",)),
        name="moe_dispatch",
    )(row_token.reshape(n_rows // tm, 1, tm), x)


def _gate_up_kernel(be_ref, x_ref, w_ref, b_ref, s_ref, o_ref, wbf_scr):
    j = pl.program_id(0)
    i = pl.program_id(1)
    changed = jnp.logical_or(i == 0, be_ref[i] != be_ref[jnp.maximum(i - 1, 0)])

    @pl.when(changed)
    def _():
        wbf_scr[...] = w_ref[0].astype(BF16)

    gu = jnp.dot(x_ref[...], wbf_scr[...], preferred_element_type=F32) + b_ref[0]
    tn = gu.shape[1]
    nxt = pltpu.roll(gu, tn - 1, 1)
    gate = jnp.minimum(gu, SWIGLU_LIMIT)
    up = jnp.clip(nxt, -SWIGLU_LIMIT, SWIGLU_LIMIT)
    act = gate * jax.nn.sigmoid(gate * SWIGLU_ALPHA) * (up + 1.0)
    o_ref[...] = jnp.dot(act.astype(BF16), s_ref[...], preferred_element_type=F32).astype(o_ref.dtype)


def expert_gate_up(xs, block_expert, w_gate_up, b_gate_up, tm=MOE_ROWS, tn=512):
    n_rows, d = xs.shape
    n_e, _, f2 = w_gate_up.shape
    nb = n_rows // tm
    sel = (jnp.arange(tn)[:, None] == 2 * jnp.arange(tn // 2)[None, :]).astype(BF16)
    grid_spec = pltpu.PrefetchScalarGridSpec(
        num_scalar_prefetch=1,
        grid=(f2 // tn, nb),
        in_specs=[pl.BlockSpec((tm, d), lambda j, i, be: (i, 0)),
                  pl.BlockSpec((1, d, tn), lambda j, i, be: (be[i], 0, j)),
                  pl.BlockSpec((1, 1, tn), lambda j, i, be: (be[i], 0, j)),
                  pl.BlockSpec((tn, tn // 2), lambda j, i, be: (0, 0))],
        out_specs=pl.BlockSpec((tm, tn // 2), lambda j, i, be: (i, j)),
        scratch_shapes=[pltpu.VMEM((d, tn), BF16)],
    )
    return pl.pallas_call(
        _gate_up_kernel,
        out_shape=jax.ShapeDtypeStruct((n_rows, f2 // 2), BF16),
        grid_spec=grid_spec,
        compiler_params=_params(("arbitrary", "arbitrary")),
        name="expert_gate_up",
    )(block_expert, xs, w_gate_up, b_gate_up.reshape(n_e, 1, f2), sel)


def _down_kernel(be_ref, a_ref, w_ref, b_ref, o_ref, wbf_scr):
    i = pl.program_id(1)
    changed = jnp.logical_or(i == 0, be_ref[i] != be_ref[jnp.maximum(i - 1, 0)])

    @pl.when(changed)
    def _():
        wbf_scr[...] = w_ref[0].astype(BF16)

    o_ref[...] = jnp.dot(a_ref[...], wbf_scr[...], preferred_element_type=F32) + b_ref[0]


def expert_down(act, block_expert, w_down, b_down, tm=MOE_ROWS, tn=1024):
    n_rows, f = act.shape
    n_e, _, d = w_down.shape
    nb = n_rows // tm
    grid_spec = pltpu.PrefetchScalarGridSpec(
        num_scalar_prefetch=1,
        grid=(d // tn, nb),
        in_specs=[pl.BlockSpec((tm, f), lambda j, i, be: (i, 0)),
                  pl.BlockSpec((1, f, tn), lambda j, i, be: (be[i], 0, j)),
                  pl.BlockSpec((1, 1, tn), lambda j, i, be: (be[i], 0, j))],
        out_specs=pl.BlockSpec((tm, tn), lambda j, i, be: (i, j)),
        scratch_shapes=[pltpu.VMEM((f, tn), BF16)],
    )
    return pl.pallas_call(
        _down_kernel,
        out_shape=jax.ShapeDtypeStruct((n_rows, d), F32),
        grid_spec=grid_spec,
        compiler_params=_params(("arbitrary", "arbitrary")),
        name="expert_down",
    )(block_expert, act, w_down, b_down.reshape(n_e, 1, d))


def _combine_kernel(pos_ref, y_hbm, h_ref, wt_ref, g_ref, h2_ref, hn_ref, buf, sem):
    tm = h_ref.shape[0]

    def copy(s):
        r = s // TOP_K
        kk = s % TOP_K
        return pltpu.make_async_copy(y_hbm.at[pl.ds(pos_ref[0, 0, s], 1), :], buf.at[kk, pl.ds(r, 1), :], sem)

    def start(s, carry):
        copy(s).start()
        return carry

    def wait(s, carry):
        copy(s).wait()
        return carry

    lax.fori_loop(0, tm * TOP_K, start, 0)
    lax.fori_loop(0, tm * TOP_K, wait, 0)
    acc = h_ref[...]
    wt = wt_ref[...]
    for kk in range(TOP_K):
        acc = acc + wt[:, kk:kk + 1] * buf[kk]
    h2_ref[...] = acc
    ms = jnp.mean(acc * acc, axis=-1, keepdims=True)
    hn_ref[...] = (acc * lax.rsqrt(ms + RMS_EPS) * g_ref[...]).astype(hn_ref.dtype)


def combine(y, pos_flat, h, wt, g_next, tm=128):
    t, d = h.shape
    tile = lambda i: (i, 0)
    return pl.pallas_call(
        _combine_kernel,
        out_shape=[jax.ShapeDtypeStruct((t, d), F32), jax.ShapeDtypeStruct((t, d), BF16)],
        grid=(t // tm,),
        in_specs=[pl.BlockSpec((1, 1, tm * TOP_K), lambda i: (i, 0, 0), memory_space=pltpu.SMEM),
                  pl.BlockSpec(memory_space=pl.ANY),
                  pl.BlockSpec((tm, d), tile),
                  pl.BlockSpec((tm, LANES), tile),
                  pl.BlockSpec((1, d), lambda i: (0, 0))],
        out_specs=[pl.BlockSpec((tm, d), tile), pl.BlockSpec((tm, d), tile)],
        scratch_shapes=[pltpu.VMEM((TOP_K, tm, d), F32), pltpu.SemaphoreType.DMA(())],
        compiler_params=_params(("arbitrary",)),
        name="moe_combine",
    )(pos_flat.reshape(t // tm, 1, tm * TOP_K), y, h, wt, g_next.reshape(1, d))


def _moe_layout(idx, rank, counts, tm):
    t = idx.shape[0]
    n_slots = t * TOP_K
    n_blocks = n_slots // tm + N_EXPERTS
    padded = (counts + tm - 1) // tm * tm
    padded_end = jnp.cumsum(padded)
    padded_start = padded_end - padded
    pos = padded_start[idx] + rank
    tok = jnp.broadcast_to(jnp.arange(t, dtype=jnp.int32)[:, None], (t, TOP_K))
    row_token = jnp.zeros((n_blocks * tm,), jnp.int32).at[pos.reshape(-1)].set(tok.reshape(-1))
    block_start = jnp.arange(n_blocks, dtype=jnp.int32) * tm
    block_expert = jnp.minimum(jnp.searchsorted(padded_end, block_start, side='right'),
                               N_EXPERTS - 1).astype(jnp.int32)
    return pos.astype(jnp.int32), row_token, block_expert


def _layer(h, p, lam_init, norm_mix, w_in, w_alpha_up, b_alpha, lq1, lk1, lq2, lk2, g_diff_subln, g_gla_out,
           w_branch, w_merge_gate, b_merge_gate, w_out, norm_moe, w_router, b_router, w_gate_up, b_gate_up,
           w_down, b_down, norm_ple, w_ple_proj, w_ple_gate, batch, seq):
    t, d = h.shape
    xn = rmsnorm(h, norm_mix, BF16)
    proj = matmul(xn, w_in[:, :N_PROJ].astype(BF16), BF16)
    w_ag = jnp.zeros((d, LANES), BF16).at[:, :GLA_GATE_RANK].set(w_in[:, N_PROJ:].astype(BF16))
    ag = matmul(xn, w_ag, F32)
    w_au = jnp.zeros((LANES, GLA_KEY_WIDTH), F32).at[:GLA_GATE_RANK].set(w_alpha_up)
    gates = matmul_bias_sigmoid(xn, w_merge_gate.astype(BF16), b_merge_gate, BF16)
    o_a = diff_attention(proj, batch, seq, lq1, lk1, lq2, lk2, g_diff_subln, lam_init)
    o_b = gla(proj, ag, w_au, b_alpha, g_gla_out, batch, seq)
    merged = merge_branches(o_a, o_b, w_branch.astype(BF16), gates)
    h = matmul_residual(merged, w_out.astype(BF16), h)
    hn, idx, wt, rank, counts = router(h, norm_moe, w_router, b_router)
    pos, row_token, block_expert = _moe_layout(idx[:, :TOP_K], rank[:, :TOP_K], counts[0, :N_EXPERTS], MOE_ROWS)
    xs = dispatch(hn, row_token)
    act = expert_gate_up(xs, block_expert, w_gate_up, b_gate_up)
    ys = expert_down(act, block_expert, w_down, b_down)
    h, hn = combine(ys, pos.reshape(-1), h, wt, norm_ple)
    h = ple_update(hn, w_ple_gate.astype(BF16), p.astype(BF16), w_ple_proj.astype(BF16), h)
    return h


def kernel(x, p, norm_mix, w_in, w_alpha_up, b_alpha, lambda_q1, lambda_k1, lambda_q2, lambda_k2, g_diff_subln, g_gla_out, w_branch, w_merge_gate, b_merge_gate, w_out, norm_moe, w_router, b_router, w_gate_up, b_gate_up, w_down, b_down, norm_ple, w_ple_proj, w_ple_gate, norm_final):
    batch, seq, d = x.shape
    depth = w_in.shape[0]
    h = x.reshape(batch * seq, d)
    for i in range(depth):
        lam_init = 0.8 - 0.6 * math.exp(-0.3 * i)
        h = _layer(h, p[i].reshape(batch * seq, -1), lam_init, norm_mix[i], w_in[i], w_alpha_up[i], b_alpha[i],
                   lambda_q1[i], lambda_k1[i], lambda_q2[i], lambda_k2[i], g_diff_subln[i], g_gla_out[i],
                   w_branch[i], w_merge_gate[i], b_merge_gate[i], w_out[i], norm_moe[i], w_router[i],
                   b_router[i], w_gate_up[i], b_gate_up[i], w_down[i], b_down[i], norm_ple[i], w_ple_proj[i],
                   w_ple_gate[i], batch, seq)
    out = rmsnorm(h, norm_final, F32)
    return out.reshape(batch, seq, d)
```

```python
import functools
import math

import jax
import jax.numpy as jnp
from jax import lax
from jax.experimental import pallas as pl
from jax.experimental.pallas import tpu as pltpu

F32 = jnp.float32
BF16 = jnp.bfloat16

RMS_EPS = 1e-6
DA_HEADS = 8
DA_HEAD_DIM = 128
DA_WIDTH = DA_HEADS * 2 * DA_HEAD_DIM
GLA_HEADS = 4
GLA_DK = 256
GLA_DV = 512
GLA_KEY_WIDTH = GLA_HEADS * GLA_DK
GLA_WIDTH = GLA_HEADS * GLA_DV
GLA_GATE_RANK = 16
GLA_TAU = 16.0
GLA_CHUNK = 64
N_PROJ = 3 * DA_WIDTH + 2 * GLA_KEY_WIDTH + 2 * GLA_WIDTH
N_EXPERTS = 32
TOP_K = 4
SWIGLU_LIMIT = 7.0
SWIGLU_ALPHA = 1.702
LANES = 128
MOE_SB = 1280
MOE_BIG = 512
MOE_RB = 128
GU_TN = 256
DN_TN = 512

VMEM_LIMIT = 56 * 1024 * 1024
MOE_VMEM_LIMIT = 60 * 1024 * 1024


def _params(sem, vmem=VMEM_LIMIT):
    return pltpu.CompilerParams(dimension_semantics=sem, vmem_limit_bytes=vmem)


def _rmsnorm_kernel(x_ref, g_ref, o_ref):
    x = x_ref[...]
    ms = jnp.mean(x * x, axis=-1, keepdims=True)
    o_ref[...] = (x * lax.rsqrt(ms + RMS_EPS) * g_ref[...]).astype(o_ref.dtype)


def rmsnorm(x, g, out_dtype, tm=256):
    t, d = x.shape
    return pl.pallas_call(
        _rmsnorm_kernel,
        out_shape=jax.ShapeDtypeStruct((t, d), out_dtype),
        grid=(t // tm,),
        in_specs=[pl.BlockSpec((tm, d), lambda i: (i, 0)),
                  pl.BlockSpec((1, d), lambda i: (0, 0))],
        out_specs=pl.BlockSpec((tm, d), lambda i: (i, 0)),
        compiler_params=_params(("parallel",)),
        name="rmsnorm",
    )(x, g.reshape(1, d))


def _mm_kernel(x_ref, w_ref, o_ref):
    o_ref[...] = jnp.dot(x_ref[...], w_ref[...], preferred_element_type=F32).astype(o_ref.dtype)


def matmul(x, w, out_dtype, tm=1024, tn=1024, n_cols=None):
    m, k = x.shape
    n = w.shape[1] if n_cols is None else n_cols
    tm, tn = min(tm, m), min(tn, n)
    return pl.pallas_call(
        _mm_kernel,
        out_shape=jax.ShapeDtypeStruct((m, n), out_dtype),
        grid=(m // tm, n // tn),
        in_specs=[pl.BlockSpec((tm, k), lambda i, j: (i, 0)),
                  pl.BlockSpec((k, tn), lambda i, j: (0, j))],
        out_specs=pl.BlockSpec((tm, tn), lambda i, j: (i, j)),
        compiler_params=_params(("parallel", "parallel")),
        name="matmul",
    )(x, w)


def _mm_wcast_kernel(x_ref, w_ref, o_ref, wbf):
    @pl.when(pl.program_id(1) == 0)
    def _():
        wbf[...] = w_ref[...].astype(BF16)

    o_ref[...] = jnp.dot(x_ref[...], wbf[...], preferred_element_type=F32).astype(o_ref.dtype)


def matmul_wcast(x, w, out_dtype, n_cols=None, tm=1024, tn=512):
    m, k = x.shape
    n = w.shape[1] if n_cols is None else n_cols
    return pl.pallas_call(
        _mm_wcast_kernel,
        out_shape=jax.ShapeDtypeStruct((m, n), out_dtype),
        grid=(n // tn, m // tm),
        in_specs=[pl.BlockSpec((tm, k), lambda j, i: (i, 0)),
                  pl.BlockSpec((k, tn), lambda j, i: (0, j))],
        out_specs=pl.BlockSpec((tm, tn), lambda j, i: (i, j)),
        scratch_shapes=[pltpu.VMEM((k, tn), BF16)],
        compiler_params=_params(("arbitrary", "arbitrary")),
        name="matmul_wcast",
    )(x, w)


def _mm_wcast_bias_sigmoid_kernel(x_ref, w_ref, b_ref, o_ref, wbf):
    @pl.when(pl.program_id(1) == 0)
    def _():
        wbf[...] = w_ref[...].astype(BF16)

    z = jnp.dot(x_ref[...], wbf[...], preferred_element_type=F32) + b_ref[...]
    o_ref[...] = jax.nn.sigmoid(z).astype(o_ref.dtype)


def matmul_wcast_bias_sigmoid(x, w, b, out_dtype, tm=1024, tn=512):
    m, k = x.shape
    n = w.shape[1]
    return pl.pallas_call(
        _mm_wcast_bias_sigmoid_kernel,
        out_shape=jax.ShapeDtypeStruct((m, n), out_dtype),
        grid=(n // tn, m // tm),
        in_specs=[pl.BlockSpec((tm, k), lambda j, i: (i, 0)),
                  pl.BlockSpec((k, tn), lambda j, i: (0, j)),
                  pl.BlockSpec((1, tn), lambda j, i: (0, j))],
        out_specs=pl.BlockSpec((tm, tn), lambda j, i: (i, j)),
        scratch_shapes=[pltpu.VMEM((k, tn), BF16)],
        compiler_params=_params(("arbitrary", "arbitrary")),
        name="matmul_wcast_bias_sigmoid",
    )(x, w, b.reshape(1, n))


def _mm_residual_kernel(x_ref, w_ref, r_ref, o_ref, wbf):
    @pl.when(pl.program_id(1) == 0)
    def _():
        wbf[...] = w_ref[...].astype(BF16)

    o_ref[...] = r_ref[...] + jnp.dot(x_ref[...], wbf[...], preferred_element_type=F32)


def matmul_residual(x, w, r, tm=1024, tn=512):
    m, k = x.shape
    n = w.shape[1]
    return pl.pallas_call(
        _mm_residual_kernel,
        out_shape=jax.ShapeDtypeStruct((m, n), F32),
        grid=(n // tn, m // tm),
        in_specs=[pl.BlockSpec((tm, k), lambda j, i: (i, 0)),
                  pl.BlockSpec((k, tn), lambda j, i: (0, j)),
                  pl.BlockSpec((tm, tn), lambda j, i: (i, j))],
        out_specs=pl.BlockSpec((tm, tn), lambda j, i: (i, j)),
        scratch_shapes=[pltpu.VMEM((k, tn), BF16)],
        compiler_params=_params(("arbitrary", "arbitrary")),
        name="matmul_residual",
    )(x, w, r)


def _merge_kernel(oa_ref, ob_ref, wa_ref, wb_ref, ga_ref, gb_ref, o_ref, wa_bf, wb_bf):
    @pl.when(pl.program_id(1) == 0)
    def _():
        wa_bf[...] = wa_ref[...].astype(BF16)
        wb_bf[...] = wb_ref[...].astype(BF16)

    ya = jnp.dot(oa_ref[...], wa_bf[...], preferred_element_type=F32)
    yb = jnp.dot(ob_ref[...], wb_bf[...], preferred_element_type=F32)
    o_ref[...] = (ga_ref[...].astype(F32) * ya + gb_ref[...].astype(F32) * yb).astype(o_ref.dtype)


def merge_branches(o_a, o_b, w_branch, gates, tm=1024, tn=512):
    m, ka = o_a.shape
    kb = o_b.shape[1]
    n = w_branch.shape[1]
    nb = n // tn
    return pl.pallas_call(
        _merge_kernel,
        out_shape=jax.ShapeDtypeStruct((m, n), BF16),
        grid=(nb, m // tm),
        in_specs=[pl.BlockSpec((tm, ka), lambda j, i: (i, 0)),
                  pl.BlockSpec((tm, kb), lambda j, i: (i, 0)),
                  pl.BlockSpec((ka, tn), lambda j, i: (0, j)),
                  pl.BlockSpec((kb, tn), lambda j, i: (1, j)),
                  pl.BlockSpec((tm, tn), lambda j, i: (i, j)),
                  pl.BlockSpec((tm, tn), lambda j, i, nb=nb: (i, j + nb))],
        out_specs=pl.BlockSpec((tm, tn), lambda j, i: (i, j)),
        scratch_shapes=[pltpu.VMEM((ka, tn), BF16), pltpu.VMEM((kb, tn), BF16)],
        compiler_params=_params(("arbitrary", "arbitrary")),
        name="merge_branches",
    )(o_a, o_b, w_branch, w_branch, gates, gates)


def _ple_kernel(hn_ref, wg_ref, p_ref, wp_ref, h_ref, o_ref, wg_bf):
    @pl.when(pl.program_id(1) == 0)
    def _():
        wg_bf[...] = wg_ref[...].astype(BF16)

    g = jax.nn.sigmoid(jnp.dot(hn_ref[...], wg_bf[...], preferred_element_type=F32))
    pp = jnp.dot(p_ref[...], wp_ref[...], preferred_element_type=F32)
    o_ref[...] = h_ref[...] + g * pp


def ple_update(hn, w_gate, p, w_proj, h, tm=1024, tn=512):
    m, k = hn.shape
    kp = p.shape[1]
    n = w_gate.shape[1]
    return pl.pallas_call(
        _ple_kernel,
        out_shape=jax.ShapeDtypeStruct((m, n), F32),
        grid=(n // tn, m // tm),
        in_specs=[pl.BlockSpec((tm, k), lambda j, i: (i, 0)),
                  pl.BlockSpec((k, tn), lambda j, i: (0, j)),
                  pl.BlockSpec((tm, kp), lambda j, i: (i, 0)),
                  pl.BlockSpec((kp, tn), lambda j, i: (0, j)),
                  pl.BlockSpec((tm, tn), lambda j, i: (i, j))],
        out_specs=pl.BlockSpec((tm, tn), lambda j, i: (i, j)),
        scratch_shapes=[pltpu.VMEM((k, tn), BF16)],
        compiler_params=_params(("arbitrary", "arbitrary")),
        name="ple_update",
    )(hn, w_gate, p, w_proj, h)


ATT_BLOCK = 512


LOG2E = 1.4426950408889634


def _diff_attn_kernel(q_ref, k_ref, v_ref, lq1_ref, lk1_ref, lq2_ref, lk2_ref, g_ref, o_ref,
                      rel_scr, m_scr, l_scr, acc_scr, *, lam_init):
    tq = q_ref.shape[0]
    d = DA_HEAD_DIM
    h = pl.program_id(1)
    i = pl.program_id(2)
    slope = LOG2E * jnp.exp2((-8.0 / DA_HEADS) * (h + 1).astype(F32) * jnp.ones((1, 1), F32))

    @pl.when(i == 0)
    def _():
        row = lax.broadcasted_iota(jnp.int32, (tq, tq), 0)
        col = lax.broadcasted_iota(jnp.int32, (tq, tq), 1)
        rel_scr[...] = (row - col).astype(F32) * slope

    q = (q_ref[...].astype(F32) * (LOG2E * d ** -0.5)).astype(BF16)
    qs = (q[:, :d], q[:, d:])

    m_scr[...] = jnp.full(m_scr.shape, -jnp.inf, F32)
    l_scr[...] = jnp.zeros(l_scr.shape, F32)
    acc_scr[...] = jnp.zeros(acc_scr.shape, F32)

    def block(j, masked):
        start = pl.multiple_of(j * tq, tq)
        vj = v_ref[pl.ds(start, tq), :]
        off = slope * ((i - j) * tq).astype(F32)
        for mp in range(2):
            kj = k_ref[pl.ds(start, tq), mp * d:(mp + 1) * d]
            s = lax.dot_general(qs[mp], kj, (((1,), (1,)), ((), ())), preferred_element_type=F32)
            t = s - rel_scr[...]
            if masked:
                row = lax.broadcasted_iota(jnp.int32, (tq, tq), 0)
                col = lax.broadcasted_iota(jnp.int32, (tq, tq), 1)
                t = jnp.where(row >= col, t, -jnp.inf)
            m_old = m_scr[mp]
            m_new = jnp.maximum(m_old, jnp.max(t, axis=-1, keepdims=True) - off)
            p = jnp.exp2(t - (m_new + off))
            alpha = jnp.exp2(m_old - m_new)
            l_scr[mp] = alpha * l_scr[mp] + jnp.sum(p, axis=-1, keepdims=True)
            acc_scr[mp] = alpha * acc_scr[mp] + jnp.dot(p.astype(BF16), vj, preferred_element_type=F32)
            m_scr[mp] = m_new

    def body(j, carry):
        block(j, False)
        return carry

    lax.fori_loop(0, i, body, 0)
    block(i, True)

    lam = (jnp.exp(jnp.sum(lq1_ref[...] * lk1_ref[...], axis=-1, keepdims=True))
           - jnp.exp(jnp.sum(lq2_ref[...] * lk2_ref[...], axis=-1, keepdims=True)) + lam_init)
    o = acc_scr[0] / l_scr[0] - lam * (acc_scr[1] / l_scr[1])
    ms = jnp.mean(o * o, axis=-1, keepdims=True)
    o = o * lax.rsqrt(ms + RMS_EPS) * g_ref[...] * (1.0 - lam_init)
    o_ref[...] = o.astype(o_ref.dtype)


def diff_attention(proj, batch, seq, lq1, lk1, lq2, lk2, g_subln, lam_init):
    tq = min(ATT_BLOCK, seq)
    nq = seq // tq
    hw = 2 * DA_HEAD_DIM
    kcol = DA_WIDTH // hw
    vcol = 2 * DA_WIDTH // hw
    vec = lambda a: a.reshape(1, -1).astype(F32)
    const = lambda b, h, i: (0, 0)
    return pl.pallas_call(
        functools.partial(_diff_attn_kernel, lam_init=lam_init),
        out_shape=jax.ShapeDtypeStruct((batch * seq, DA_WIDTH), BF16),
        grid=(batch, DA_HEADS, nq),
        in_specs=[pl.BlockSpec((tq, hw), lambda b, h, i: (b * nq + i, h)),
                  pl.BlockSpec((seq, hw), lambda b, h, i: (b, kcol + h)),
                  pl.BlockSpec((seq, hw), lambda b, h, i: (b, vcol + h)),
                  pl.BlockSpec((1, DA_HEAD_DIM), const),
                  pl.BlockSpec((1, DA_HEAD_DIM), const),
                  pl.BlockSpec((1, DA_HEAD_DIM), const),
                  pl.BlockSpec((1, DA_HEAD_DIM), const),
                  pl.BlockSpec((1, hw), const)],
        out_specs=pl.BlockSpec((tq, hw), lambda b, h, i: (b * nq + i, h)),
        scratch_shapes=[pltpu.VMEM((tq, tq), F32),
                        pltpu.VMEM((2, tq, 1), F32),
                        pltpu.VMEM((2, tq, 1), F32),
                        pltpu.VMEM((2, tq, hw), F32)],
        compiler_params=_params(("arbitrary", "arbitrary", "arbitrary")),
        name="diff_attention",
    )(proj, proj, proj, vec(lq1), vec(lk1), vec(lq2), vec(lk2), vec(g_subln))


GLA_STEP = 512


def _gla_kernel(q_ref, k_ref, v_ref, og_ref, ag_ref, wa_ref, ba_ref, g_ref, o_ref, state_scr):
    c = GLA_CHUNK
    n_chunks = q_ref.shape[0] // c

    @pl.when(pl.program_id(2) == 0)
    def _():
        state_scr[...] = jnp.zeros(state_scr.shape, F32)

    row = lax.broadcasted_iota(jnp.int32, (c, c), 0)
    col = lax.broadcasted_iota(jnp.int32, (c, c), 1)
    tril = row >= col
    tri_ones = tril.astype(F32)
    hi = lax.Precision.HIGHEST

    for n in range(n_chunks):
        sl = pl.ds(n * c, c)
        z = jnp.dot(ag_ref[sl, :], wa_ref[...], precision=hi, preferred_element_type=F32) + ba_ref[...]
        log_a = jax.nn.log_sigmoid(z) * (1.0 / GLA_TAU)
        b = jnp.dot(tri_ones, log_a, precision=hi, preferred_element_type=F32)
        b_mid = b[c // 2:c // 2 + 1, :]
        b_last = b[c - 1:c, :]
        q = q_ref[sl, :].astype(F32) * (GLA_DK ** -0.5)
        k = k_ref[sl, :].astype(F32)
        v = v_ref[sl, :]
        qg = (q * jnp.exp(b - b_mid)).astype(BF16)
        kg = (k * jnp.exp(b_mid - b)).astype(BF16)
        att = lax.dot_general(qg, kg, (((1,), (1,)), ((), ())), preferred_element_type=F32)
        att = jnp.where(tril, att, 0.0).astype(BF16)
        o = jnp.dot(att, v, preferred_element_type=F32)
        q_inter = (q * jnp.exp(b)).astype(BF16)
        state = state_scr[...]
        o = o + lax.dot_general(q_inter, state.astype(BF16), (((1,), (1,)), ((), ())),
                                preferred_element_type=F32)
        k_state = (k * jnp.exp(b_last - b)).astype(BF16)
        kv = lax.dot_general(v, k_state, (((0,), (0,)), ((), ())), preferred_element_type=F32)
        state_scr[...] = state * jnp.exp(b_last) + kv
        ms = jnp.mean(o * o, axis=-1, keepdims=True)
        og = og_ref[sl, :].astype(F32)
        o = o * lax.rsqrt(ms + RMS_EPS) * g_ref[...] * (og * jax.nn.sigmoid(og))
        o_ref[sl, :] = o.astype(o_ref.dtype)


def gla(proj, ag, w_alpha_up, b_alpha, g_out, batch, seq):
    ts = min(GLA_STEP, seq)
    ns = seq // ts
    qcol = 3 * DA_WIDTH // GLA_DK
    kcol = qcol + GLA_KEY_WIDTH // GLA_DK
    vcol = (3 * DA_WIDTH + 2 * GLA_KEY_WIDTH) // GLA_DV
    ocol = vcol + GLA_WIDTH // GLA_DV
    rank = ag.shape[1]
    return pl.pallas_call(
        _gla_kernel,
        out_shape=jax.ShapeDtypeStruct((batch * seq, GLA_WIDTH), BF16),
        grid=(batch, GLA_HEADS, ns),
        in_specs=[pl.BlockSpec((ts, GLA_DK), lambda b, h, i: (b * ns + i, qcol + h)),
                  pl.BlockSpec((ts, GLA_DK), lambda b, h, i: (b * ns + i, kcol + h)),
                  pl.BlockSpec((ts, GLA_DV), lambda b, h, i: (b * ns + i, vcol + h)),
                  pl.BlockSpec((ts, GLA_DV), lambda b, h, i: (b * ns + i, ocol + h)),
                  pl.BlockSpec((ts, rank), lambda b, h, i: (b * ns + i, 0)),
                  pl.BlockSpec((rank, GLA_DK), lambda b, h, i: (0, h)),
                  pl.BlockSpec((1, GLA_DK), lambda b, h, i: (0, h)),
                  pl.BlockSpec((1, GLA_DV), lambda b, h, i: (0, 0))],
        out_specs=pl.BlockSpec((ts, GLA_DV), lambda b, h, i: (b * ns + i, h)),
        scratch_shapes=[pltpu.VMEM((GLA_DV, GLA_DK), F32)],
        compiler_params=_params(("parallel", "parallel", "arbitrary")),
        name="gla",
    )(proj, proj, proj, proj, ag, w_alpha_up, b_alpha.reshape(1, -1), g_out.reshape(1, -1))


def _router_kernel(h_ref, g_ref, w_ref, b_ref, hn_ref, idx_ref, wt_ref, rank_ref, cnt_ref, cnt_scr):
    tm = h_ref.shape[0]

    @pl.when(pl.program_id(0) == 0)
    def _():
        cnt_scr[...] = jnp.zeros(cnt_scr.shape, F32)

    x = h_ref[...]
    ms = jnp.mean(x * x, axis=-1, keepdims=True)
    hn = x * lax.rsqrt(ms + RMS_EPS) * g_ref[...]
    hn_ref[...] = hn
    logits = jnp.dot(hn, w_ref[...], precision=lax.Precision.HIGHEST, preferred_element_type=F32) + b_ref[...]
    lane_i = lax.broadcasted_iota(jnp.int32, (tm, LANES), 1)
    lane = lane_i.astype(F32)
    logits = jnp.where(lane_i < N_EXPERTS, logits, -jnp.inf)

    row = lax.broadcasted_iota(jnp.int32, (tm, tm), 0)
    col = lax.broadcasted_iota(jnp.int32, (tm, tm), 1)
    strict = (row > col).astype(BF16)

    work = logits
    sel = jnp.zeros((tm, LANES), jnp.bool_)
    idx_out = jnp.zeros((tm, LANES), F32)
    val_out = jnp.full((tm, LANES), -jnp.inf, F32)
    onehots = []
    for kk in range(TOP_K):
        mx = jnp.max(work, axis=-1, keepdims=True)
        idx = jnp.min(jnp.where(work == mx, lane, float(LANES)), axis=-1, keepdims=True)
        onehot = lane == idx
        onehots.append(onehot)
        sel = jnp.logical_or(sel, onehot)
        idx_out = jnp.where(lane_i == kk, idx, idx_out)
        val_out = jnp.where(lane_i == kk, mx, val_out)
        work = jnp.where(onehot, -jnp.inf, work)

    e = jnp.exp(val_out - jnp.max(val_out, axis=-1, keepdims=True))
    wt_ref[...] = e / jnp.sum(e, axis=-1, keepdims=True)
    idx_ref[...] = idx_out.astype(jnp.int32)

    self = sel.astype(F32)
    rank_dense = jnp.dot(strict, self.astype(BF16), preferred_element_type=F32) + cnt_scr[...]
    rank_out = jnp.zeros((tm, LANES), F32)
    for kk in range(TOP_K):
        r = jnp.sum(jnp.where(onehots[kk], rank_dense, 0.0), axis=-1, keepdims=True)
        rank_out = jnp.where(lane_i == kk, r, rank_out)
    rank_ref[...] = rank_out.astype(jnp.int32)
    cnt_scr[...] = cnt_scr[...] + jnp.sum(self, axis=0, keepdims=True)
    cnt_ref[...] = cnt_scr[...].astype(jnp.int32)


def router(h, g, w_router, b_router, tm=256):
    t, d = h.shape
    wpad = jnp.zeros((d, LANES), F32).at[:, :N_EXPERTS].set(w_router)
    bpad = jnp.zeros((1, LANES), F32).at[0, :N_EXPERTS].set(b_router)
    tile = lambda i: (i, 0)
    const = lambda i: (0, 0)
    return pl.pallas_call(
        _router_kernel,
        out_shape=[jax.ShapeDtypeStruct((t, d), F32),
                   jax.ShapeDtypeStruct((t, LANES), jnp.int32),
                   jax.ShapeDtypeStruct((t, LANES), F32),
                   jax.ShapeDtypeStruct((t, LANES), jnp.int32),
                   jax.ShapeDtypeStruct((1, LANES), jnp.int32)],
        grid=(t // tm,),
        in_specs=[pl.BlockSpec((tm, d), tile), pl.BlockSpec((1, d), const),
                  pl.BlockSpec((d, LANES), const), pl.BlockSpec((1, LANES), const)],
        out_specs=[pl.BlockSpec((tm, d), tile), pl.BlockSpec((tm, LANES), tile),
                   pl.BlockSpec((tm, LANES), tile), pl.BlockSpec((tm, LANES), tile),
                   pl.BlockSpec((1, LANES), const)],
        scratch_shapes=[pltpu.VMEM((1, LANES), F32)],
        compiler_params=_params(("arbitrary",)),
        name="router",
    )(h, g.reshape(1, d), wpad, bpad)


def _expert_kernel(sbe_ref, sbc_ref, nv_ref, tok_ref, tokn_ref, hn_hbm, wgu_ref, bgu_ref, sel_ref, wd_ref, bd_ref,
                   y_ref, x2d, stage, act, wgu_bf, wd_bf, sems, *, n1):
    g = pl.program_id(0)
    s = pl.program_id(1)
    n_sb = pl.num_programs(0)
    n_steps = pl.num_programs(1)
    nv = nv_ref[0]
    valid = g < nv
    cnt = sbc_ref[g]
    cnt_next = jnp.where(g + 1 < nv, sbc_ref[jnp.minimum(g + 1, n_sb - 1)], 0)
    cur = lax.rem(g, 2)
    rb = MOE_RB
    big = MOE_BIG
    ch = stage.shape[1]
    n_rb = x2d.shape[1] // rb
    ha = act.shape[2]

    def chunk_rows(c, total):
        n = jnp.clip(total - c * ch, 0, ch)
        return pl.multiple_of(lax.shift_left(lax.shift_right_logical(n + 7, 3), 3), 8)

    def issue_chunk(tref, c, total, slot):
        def body(r, carry):
            pltpu.make_async_copy(hn_hbm.at[pl.ds(tref[0, 0, c * ch + r], 1), :],
                                  stage.at[slot, pl.ds(r, 1), :], sems.at[slot]).start()
            return carry
        lax.fori_loop(0, chunk_rows(c, total), body, 0)

    def finish_chunk(c, total, slot, xslot):
        n = chunk_rows(c, total)

        @pl.when(n > 0)
        def _():
            pltpu.make_async_copy(hn_hbm.at[pl.ds(0, n), :], stage.at[slot, pl.ds(0, n), :], sems.at[slot]).wait()

        x2d[xslot, pl.ds(pl.multiple_of(c * ch, ch), ch), :] = stage[slot].astype(BF16)

    @pl.when(jnp.logical_and(g == 0, s == 0))
    def _():
        stage[...] = jnp.zeros(stage.shape, stage.dtype)

        def body(c, carry):
            issue_chunk(tok_ref, c, cnt, 0)
            finish_chunk(c, cnt, 0, 0)
            return carry
        lax.fori_loop(0, n_steps, body, 0)

    @pl.when(jnp.logical_and(jnp.logical_and(valid, g > 0), s == 0))
    def _():
        finish_chunk(n_steps - 1, cnt, lax.rem(n_steps - 1, 2), cur)

    @pl.when(cnt_next > 0)
    def _():
        @pl.when(s > 0)
        def _():
            finish_chunk(s - 1, cnt_next, lax.rem(s - 1, 2), 1 - cur)
        issue_chunk(tokn_ref, s, cnt_next, lax.rem(s, 2))

    n_big = lax.shift_right_logical(cnt, big.bit_length() - 1)
    n_small = lax.shift_right_logical(cnt - n_big * big + (rb - 1), rb.bit_length() - 1)

    def for_row_blocks(fn):
        def big_body(i, carry):
            fn(pl.multiple_of(i * big, big), big)
            return carry

        def small_body(i, carry):
            fn(pl.multiple_of(n_big * big + i * rb, rb), rb)
            return carry

        lax.fori_loop(0, n_big, big_body, 0)
        lax.fori_loop(0, n_small, small_body, 0)

    @pl.when(jnp.logical_and(valid, s < n1))
    def _():
        wgu_bf[...] = wgu_ref[0].astype(BF16)
        tn = wgu_bf.shape[1]

        def sub(start, size):
            rows = pl.ds(start, size)
            gu = jnp.dot(x2d[cur, rows, :], wgu_bf[...], preferred_element_type=F32) + bgu_ref[0]
            nxt = pltpu.roll(gu, tn - 1, 1)
            gate = jnp.minimum(gu, SWIGLU_LIMIT)
            up = jnp.clip(nxt, -SWIGLU_LIMIT, SWIGLU_LIMIT)
            a = gate * jax.nn.sigmoid(gate * SWIGLU_ALPHA) * (up + 1.0)
            act[s, rows, :] = jnp.dot(a.astype(BF16), sel_ref[...], preferred_element_type=F32).astype(BF16)

        for_row_blocks(sub)

    @pl.when(jnp.logical_and(valid, s >= n1))
    def _():
        wd_bf[...] = wd_ref[0].astype(BF16)
        tn2 = y_ref.shape[1]

        def sub(start, size):
            rows = pl.ds(start, size)
            y = bd_ref[0]
            for jj in range(0, n1, 2):
                a2 = jnp.concatenate([act[jj, rows, :], act[jj + 1, rows, :]], axis=1)
                y = y + jnp.dot(a2, wd_bf[jj * ha:(jj + 2) * ha, :], preferred_element_type=F32)
            y_ref[rows, :] = y

        def zero(i, carry):
            y_ref[pl.ds(pl.multiple_of(i * rb, rb), rb), :] = jnp.zeros((rb, tn2), y_ref.dtype)
            return carry

        for_row_blocks(sub)
        lax.fori_loop((big // rb) * n_big + n_small, n_rb, zero, 0)

    @pl.when(jnp.logical_and(jnp.logical_not(valid), s >= n1))
    def _():
        y_ref[...] = jnp.zeros(y_ref.shape, y_ref.dtype)


def expert_ffn(hn, tok_table, sb_expert, sb_count, n_valid, w_gate_up, b_gate_up, w_down, b_down):
    n_e, d, f2 = w_gate_up.shape
    f = f2 // 2
    r = MOE_SB
    n_sb = sb_expert.shape[0]
    n1, n2 = f2 // GU_TN, d // DN_TN
    assert n1 % 2 == 0 and r % (n1 + n2) == 0
    ch = r // (n1 + n2)
    ha = GU_TN // 2
    sel = (jnp.arange(GU_TN)[:, None] == 2 * jnp.arange(ha)[None, :]).astype(BF16)

    def cur(g, nv):
        return jnp.maximum(jnp.minimum(g, nv[0] - 1), 0)

    def nxt(g, nv):
        return jnp.maximum(jnp.minimum(g + 1, nv[0] - 1), 0)

    def gu_idx(g, s, sbe, sbc, nv):
        in_gu = jnp.logical_and(g < nv[0], s < n1)
        return (jnp.where(in_gu, sbe[cur(g, nv)], sbe[nxt(g, nv)]), 0, jnp.where(in_gu, s, 0))

    def dn_idx(g, s, sbe, sbc, nv):
        in_dn = jnp.logical_and(g < nv[0], s >= n1)
        return (sbe[cur(g, nv)], 0, jnp.where(in_dn, s - n1, 0))

    def y_idx(g, s, sbe, sbc, nv):
        return (g, jnp.maximum(s - n1, 0))

    grid_spec = pltpu.PrefetchScalarGridSpec(
        num_scalar_prefetch=3,
        grid=(n_sb, n1 + n2),
        in_specs=[pl.BlockSpec((1, 1, r), lambda g, s, sbe, sbc, nv: (cur(g, nv), 0, 0), memory_space=pltpu.SMEM),
                  pl.BlockSpec((1, 1, r), lambda g, s, sbe, sbc, nv: (nxt(g, nv), 0, 0), memory_space=pltpu.SMEM),
                  pl.BlockSpec(memory_space=pl.ANY),
                  pl.BlockSpec((1, d, GU_TN), gu_idx),
                  pl.BlockSpec((1, 1, GU_TN), gu_idx),
                  pl.BlockSpec((GU_TN, ha), lambda g, s, sbe, sbc, nv: (0, 0)),
                  pl.BlockSpec((1, f, DN_TN), dn_idx),
                  pl.BlockSpec((1, 1, DN_TN), dn_idx)],
        out_specs=pl.BlockSpec((r, DN_TN), y_idx),
        scratch_shapes=[pltpu.VMEM((2, r, d), BF16),
                        pltpu.VMEM((2, ch, d), F32),
                        pltpu.VMEM((n1, r, ha), BF16),
                        pltpu.VMEM((d, GU_TN), BF16),
                        pltpu.VMEM((f, DN_TN), BF16),
                        pltpu.SemaphoreType.DMA((2,))],
    )
    tok3 = tok_table.reshape(n_sb, 1, r)
    return pl.pallas_call(
        functools.partial(_expert_kernel, n1=n1),
        out_shape=jax.ShapeDtypeStruct((n_sb * r, d), F32),
        grid_spec=grid_spec,
        compiler_params=_params(("arbitrary", "arbitrary"), vmem=MOE_VMEM_LIMIT),
        name="expert_ffn",
    )(sb_expert, sb_count, n_valid, tok3, tok3, hn, w_gate_up, b_gate_up.reshape(n_e, 1, f2), sel,
      w_down, b_down.reshape(n_e, 1, d))


def _combine_kernel(pos_ref, posn_ref, y_hbm, h_ref, wt_ref, g_ref, h2_ref, hn_ref, buf, sems):
    tm = h_ref.shape[0]
    i = pl.program_id(0)
    slot = lax.rem(i, 2)

    def gather(pref, sl, start):
        def body(s, carry):
            r = lax.shift_right_logical(s, TOP_K.bit_length() - 1)
            kk = lax.rem(s, TOP_K)
            cp = pltpu.make_async_copy(y_hbm.at[pl.ds(pref[0, 0, s], 1), :], buf.at[sl, kk, pl.ds(r, 1), :],
                                       sems.at[sl])
            if start:
                cp.start()
            else:
                cp.wait()
            return carry
        lax.fori_loop(0, tm * TOP_K, body, 0)

    @pl.when(i == 0)
    def _():
        gather(pos_ref, slot, True)

    @pl.when(i + 1 < pl.num_programs(0))
    def _():
        gather(posn_ref, 1 - slot, True)

    gather(pos_ref, slot, False)
    acc = h_ref[...]
    wt = wt_ref[...]
    for kk in range(TOP_K):
        acc = acc + wt[:, kk:kk + 1] * buf[slot, kk]
    h2_ref[...] = acc
    ms = jnp.mean(acc * acc, axis=-1, keepdims=True)
    hn_ref[...] = (acc * lax.rsqrt(ms + RMS_EPS) * g_ref[...]).astype(hn_ref.dtype)


def combine(y, pos_flat, h, wt, g_next, tm=128):
    t, d = h.shape
    nt = t // tm
    tile = lambda i: (i, 0)
    pos3 = pos_flat.reshape(nt, 1, tm * TOP_K)
    return pl.pallas_call(
        _combine_kernel,
        out_shape=[jax.ShapeDtypeStruct((t, d), F32), jax.ShapeDtypeStruct((t, d), BF16)],
        grid=(nt,),
        in_specs=[pl.BlockSpec((1, 1, tm * TOP_K), lambda i: (i, 0, 0), memory_space=pltpu.SMEM),
                  pl.BlockSpec((1, 1, tm * TOP_K), lambda i: (jnp.minimum(i + 1, nt - 1), 0, 0),
                               memory_space=pltpu.SMEM),
                  pl.BlockSpec(memory_space=pl.ANY),
                  pl.BlockSpec((tm, d), tile),
                  pl.BlockSpec((tm, LANES), tile),
                  pl.BlockSpec((1, d), lambda i: (0, 0))],
        out_specs=[pl.BlockSpec((tm, d), tile), pl.BlockSpec((tm, d), tile)],
        scratch_shapes=[pltpu.VMEM((2, TOP_K, tm, d), F32), pltpu.SemaphoreType.DMA((2,))],
        compiler_params=_params(("arbitrary",)),
        name="moe_combine",
    )(pos3, pos3, y, h, wt, g_next.reshape(1, d))


def _moe_layout(idx, rank, counts):
    t = idx.shape[0]
    r = MOE_SB
    n_sb = N_EXPERTS + -(-(t * TOP_K) // r)
    nsb = (counts + r - 1) // r
    sb_end = jnp.cumsum(nsb)
    sb_first = sb_end - nsb
    n_valid = sb_end[-1:]
    pos = sb_first[idx] * r + rank
    tok = jnp.broadcast_to(jnp.arange(t, dtype=jnp.int32)[:, None], (t, TOP_K))
    tok_table = jnp.zeros((n_sb * r,), jnp.int32).at[pos.reshape(-1)].set(tok.reshape(-1))
    g = jnp.arange(n_sb, dtype=jnp.int32)
    gv = jnp.minimum(g, n_valid - 1)
    sb_expert = jnp.sum(gv[:, None] >= sb_end[None, :], axis=1).astype(jnp.int32)
    sb_count = jnp.clip(counts[sb_expert] - (gv - sb_first[sb_expert]) * r, 0, r)
    sb_count = jnp.where(g < n_valid, sb_count, 0)
    i32 = lambda a: a.astype(jnp.int32)
    return i32(pos), tok_table, i32(sb_expert), i32(sb_count), i32(n_valid)


def _layer(h, p, lam_init, norm_mix, w_in, w_alpha_up, b_alpha, lq1, lk1, lq2, lk2, g_diff_subln, g_gla_out,
           w_branch, w_merge_gate, b_merge_gate, w_out, norm_moe, w_router, b_router, w_gate_up, b_gate_up,
           w_down, b_down, norm_ple, w_ple_proj, w_ple_gate, batch, seq):
    t, d = h.shape
    xn = rmsnorm(h, norm_mix, BF16)
    proj = matmul_wcast(xn, w_in, BF16, n_cols=N_PROJ)
    w_ag = jnp.zeros((d, LANES), BF16).at[:, :GLA_GATE_RANK].set(w_in[:, N_PROJ:].astype(BF16))
    ag = matmul(xn, w_ag, F32)
    w_au = jnp.zeros((LANES, GLA_KEY_WIDTH), F32).at[:GLA_GATE_RANK].set(w_alpha_up)
    gates = matmul_wcast_bias_sigmoid(xn, w_merge_gate, b_merge_gate, BF16)
    o_a = diff_attention(proj, batch, seq, lq1, lk1, lq2, lk2, g_diff_subln, lam_init)
    o_b = gla(proj, ag, w_au, b_alpha, g_gla_out, batch, seq)
    merged = merge_branches(o_a, o_b, w_branch, gates)
    h = matmul_residual(merged, w_out, h)
    hn, idx, wt, rank, counts = router(h, norm_moe, w_router, b_router)
    pos, tok_table, sb_expert, sb_count, n_valid = _moe_layout(idx[:, :TOP_K], rank[:, :TOP_K],
                                                               counts[0, :N_EXPERTS])
    ys = expert_ffn(hn, tok_table, sb_expert, sb_count, n_valid, w_gate_up, b_gate_up, w_down, b_down)
    h, hn = combine(ys, pos.reshape(-1), h, wt, norm_ple)
    h = ple_update(hn, w_ple_gate, p.astype(BF16), w_ple_proj.astype(BF16), h)
    return h


def kernel(x, p, norm_mix, w_in, w_alpha_up, b_alpha, lambda_q1, lambda_k1, lambda_q2, lambda_k2, g_diff_subln, g_gla_out, w_branch, w_merge_gate, b_merge_gate, w_out, norm_moe, w_router, b_router, w_gate_up, b_gate_up, w_down, b_down, norm_ple, w_ple_proj, w_ple_gate, norm_final):
    batch, seq, d = x.shape
    depth = w_in.shape[0]
    h = x.reshape(batch * seq, d)
    for i in range(depth):
        lam_init = 0.8 - 0.6 * math.exp(-0.3 * i)
        h = _layer(h, p[i].reshape(batch * seq, -1), lam_init, norm_mix[i], w_in[i], w_alpha_up[i], b_alpha[i],
                   lambda_q1[i], lambda_k1[i], lambda_q2[i], lambda_k2[i], g_diff_subln[i], g_gla_out[i],
                   w_branch[i], w_merge_gate[i], b_merge_gate[i], w_out[i], norm_moe[i], w_router[i],
                   b_router[i], w_gate_up[i], b_gate_up[i], w_down[i], b_down[i], norm_ple[i], w_ple_proj[i],
                   w_ple_gate[i], batch, seq)
    out = rmsnorm(h, norm_final, F32)
    return out.reshape(batch, seq, d)
```

```python
import functools
import math

import jax
import jax.numpy as jnp
from jax import lax
from jax.experimental import pallas as pl
from jax.experimental.pallas import tpu as pltpu

F32 = jnp.float32
BF16 = jnp.bfloat16

RMS_EPS = 1e-6
DA_HEADS = 8
DA_HEAD_DIM = 128
DA_WIDTH = DA_HEADS * 2 * DA_HEAD_DIM
GLA_HEADS = 4
GLA_DK = 256
GLA_DV = 512
GLA_KEY_WIDTH = GLA_HEADS * GLA_DK
GLA_WIDTH = GLA_HEADS * GLA_DV
GLA_GATE_RANK = 16
GLA_TAU = 16.0
GLA_CHUNK = 64
N_PROJ = 3 * DA_WIDTH + 2 * GLA_KEY_WIDTH + 2 * GLA_WIDTH
N_EXPERTS = 32
TOP_K = 4
SWIGLU_LIMIT = 7.0
SWIGLU_ALPHA = 1.702
LANES = 128
MOE_SB = 1280
MOE_BIG = 512
MOE_RB = 128
GU_TN = 256
DN_TN = 512

VMEM_LIMIT = 56 * 1024 * 1024
MOE_VMEM_LIMIT = 60 * 1024 * 1024


def _params(sem, vmem=VMEM_LIMIT):
    return pltpu.CompilerParams(dimension_semantics=sem, vmem_limit_bytes=vmem)


def _rmsnorm_kernel(x_ref, g_ref, o_ref):
    x = x_ref[...]
    ms = jnp.mean(x * x, axis=-1, keepdims=True)
    o_ref[...] = (x * lax.rsqrt(ms + RMS_EPS) * g_ref[...]).astype(o_ref.dtype)


def rmsnorm(x, g, out_dtype, tm=256):
    t, d = x.shape
    return pl.pallas_call(
        _rmsnorm_kernel,
        out_shape=jax.ShapeDtypeStruct((t, d), out_dtype),
        grid=(t // tm,),
        in_specs=[pl.BlockSpec((tm, d), lambda i: (i, 0)),
                  pl.BlockSpec((1, d), lambda i: (0, 0))],
        out_specs=pl.BlockSpec((tm, d), lambda i: (i, 0)),
        compiler_params=_params(("parallel",)),
        name="rmsnorm",
    )(x, g.reshape(1, d))


def _mm_kernel(x_ref, w_ref, o_ref):
    o_ref[...] = jnp.dot(x_ref[...], w_ref[...], preferred_element_type=F32).astype(o_ref.dtype)


def matmul(x, w, out_dtype, tm=1024, tn=1024, n_cols=None):
    m, k = x.shape
    n = w.shape[1] if n_cols is None else n_cols
    tm, tn = min(tm, m), min(tn, n)
    return pl.pallas_call(
        _mm_kernel,
        out_shape=jax.ShapeDtypeStruct((m, n), out_dtype),
        grid=(m // tm, n // tn),
        in_specs=[pl.BlockSpec((tm, k), lambda i, j: (i, 0)),
                  pl.BlockSpec((k, tn), lambda i, j: (0, j))],
        out_specs=pl.BlockSpec((tm, tn), lambda i, j: (i, j)),
        compiler_params=_params(("parallel", "parallel")),
        name="matmul",
    )(x, w)


def _mm_wcast_kernel(x_ref, w_ref, o_ref, wbf):
    @pl.when(pl.program_id(1) == 0)
    def _():
        wbf[...] = w_ref[...].astype(BF16)

    o_ref[...] = jnp.dot(x_ref[...], wbf[...], preferred_element_type=F32).astype(o_ref.dtype)


def matmul_wcast(x, w, layer, out_dtype, n_cols=None, tm=1024, tn=512):
    m, k = x.shape
    n = w.shape[2] if n_cols is None else n_cols
    return pl.pallas_call(
        _mm_wcast_kernel,
        out_shape=jax.ShapeDtypeStruct((m, n), out_dtype),
        grid=(n // tn, m // tm),
        in_specs=[pl.BlockSpec((tm, k), lambda j, i: (i, 0)),
                  pl.BlockSpec((None, k, tn), lambda j, i: (layer, 0, j))],
        out_specs=pl.BlockSpec((tm, tn), lambda j, i: (i, j)),
        scratch_shapes=[pltpu.VMEM((k, tn), BF16)],
        compiler_params=_params(("arbitrary", "arbitrary")),
        name="matmul_wcast",
    )(x, w)


def _mm_wcast_bias_sigmoid_kernel(x_ref, w_ref, b_ref, o_ref, wbf):
    @pl.when(pl.program_id(1) == 0)
    def _():
        wbf[...] = w_ref[...].astype(BF16)

    z = jnp.dot(x_ref[...], wbf[...], preferred_element_type=F32) + b_ref[...]
    o_ref[...] = jax.nn.sigmoid(z).astype(o_ref.dtype)


def matmul_wcast_bias_sigmoid(x, w, b, out_dtype, tm=1024, tn=512):
    m, k = x.shape
    n = w.shape[1]
    return pl.pallas_call(
        _mm_wcast_bias_sigmoid_kernel,
        out_shape=jax.ShapeDtypeStruct((m, n), out_dtype),
        grid=(n // tn, m // tm),
        in_specs=[pl.BlockSpec((tm, k), lambda j, i: (i, 0)),
                  pl.BlockSpec((k, tn), lambda j, i: (0, j)),
                  pl.BlockSpec((1, tn), lambda j, i: (0, j))],
        out_specs=pl.BlockSpec((tm, tn), lambda j, i: (i, j)),
        scratch_shapes=[pltpu.VMEM((k, tn), BF16)],
        compiler_params=_params(("arbitrary", "arbitrary")),
        name="matmul_wcast_bias_sigmoid",
    )(x, w, b.reshape(1, n))


def _mm_residual_kernel(x_ref, w_ref, r_ref, o_ref, wbf):
    @pl.when(pl.program_id(1) == 0)
    def _():
        wbf[...] = w_ref[...].astype(BF16)

    o_ref[...] = r_ref[...] + jnp.dot(x_ref[...], wbf[...], preferred_element_type=F32)


def matmul_residual(x, w, r, tm=1024, tn=512):
    m, k = x.shape
    n = w.shape[1]
    return pl.pallas_call(
        _mm_residual_kernel,
        out_shape=jax.ShapeDtypeStruct((m, n), F32),
        grid=(n // tn, m // tm),
        in_specs=[pl.BlockSpec((tm, k), lambda j, i: (i, 0)),
                  pl.BlockSpec((k, tn), lambda j, i: (0, j)),
                  pl.BlockSpec((tm, tn), lambda j, i: (i, j))],
        out_specs=pl.BlockSpec((tm, tn), lambda j, i: (i, j)),
        scratch_shapes=[pltpu.VMEM((k, tn), BF16)],
        compiler_params=_params(("arbitrary", "arbitrary")),
        name="matmul_residual",
    )(x, w, r)


def _merge_kernel(oa_ref, ob_ref, wa_ref, wb_ref, ga_ref, gb_ref, o_ref, wa_bf, wb_bf):
    @pl.when(pl.program_id(1) == 0)
    def _():
        wa_bf[...] = wa_ref[...].astype(BF16)
        wb_bf[...] = wb_ref[...].astype(BF16)

    ya = jnp.dot(oa_ref[...], wa_bf[...], preferred_element_type=F32)
    yb = jnp.dot(ob_ref[...], wb_bf[...], preferred_element_type=F32)
    o_ref[...] = (ga_ref[...].astype(F32) * ya + gb_ref[...].astype(F32) * yb).astype(o_ref.dtype)


def merge_branches(o_a, o_b, w_branch, gates, tm=1024, tn=512):
    m, ka = o_a.shape
    kb = o_b.shape[1]
    n = w_branch.shape[1]
    nb = n // tn
    return pl.pallas_call(
        _merge_kernel,
        out_shape=jax.ShapeDtypeStruct((m, n), BF16),
        grid=(nb, m // tm),
        in_specs=[pl.BlockSpec((tm, ka), lambda j, i: (i, 0)),
                  pl.BlockSpec((tm, kb), lambda j, i: (i, 0)),
                  pl.BlockSpec((ka, tn), lambda j, i: (0, j)),
                  pl.BlockSpec((kb, tn), lambda j, i: (1, j)),
                  pl.BlockSpec((tm, tn), lambda j, i: (i, j)),
                  pl.BlockSpec((tm, tn), lambda j, i, nb=nb: (i, j + nb))],
        out_specs=pl.BlockSpec((tm, tn), lambda j, i: (i, j)),
        scratch_shapes=[pltpu.VMEM((ka, tn), BF16), pltpu.VMEM((kb, tn), BF16)],
        compiler_params=_params(("arbitrary", "arbitrary")),
        name="merge_branches",
    )(o_a, o_b, w_branch, w_branch, gates, gates)


def _ple_kernel(hn_ref, wg_ref, p_ref, wp_ref, h_ref, o_ref, wg_bf):
    @pl.when(pl.program_id(1) == 0)
    def _():
        wg_bf[...] = wg_ref[...].astype(BF16)

    g = jax.nn.sigmoid(jnp.dot(hn_ref[...], wg_bf[...], preferred_element_type=F32))
    pp = jnp.dot(p_ref[...], wp_ref[...], preferred_element_type=F32)
    o_ref[...] = h_ref[...] + g * pp


def ple_update(hn, w_gate, p, w_proj, h, tm=1024, tn=512):
    m, k = hn.shape
    kp = p.shape[1]
    n = w_gate.shape[1]
    return pl.pallas_call(
        _ple_kernel,
        out_shape=jax.ShapeDtypeStruct((m, n), F32),
        grid=(n // tn, m // tm),
        in_specs=[pl.BlockSpec((tm, k), lambda j, i: (i, 0)),
                  pl.BlockSpec((k, tn), lambda j, i: (0, j)),
                  pl.BlockSpec((tm, kp), lambda j, i: (i, 0)),
                  pl.BlockSpec((kp, tn), lambda j, i: (0, j)),
                  pl.BlockSpec((tm, tn), lambda j, i: (i, j))],
        out_specs=pl.BlockSpec((tm, tn), lambda j, i: (i, j)),
        scratch_shapes=[pltpu.VMEM((k, tn), BF16)],
        compiler_params=_params(("arbitrary", "arbitrary")),
        name="ple_update",
    )(hn, w_gate, p, w_proj, h)


ATT_BLOCK = 512


LOG2E = 1.4426950408889634


def _diff_attn_kernel(q_ref, k_ref, v_ref, lq1_ref, lk1_ref, lq2_ref, lk2_ref, g_ref, o_ref,
                      rel_scr, m_scr, l_scr, acc_scr, *, lam_init):
    tq = q_ref.shape[0]
    d = DA_HEAD_DIM
    h = pl.program_id(1)
    i = pl.program_id(2)
    slope = LOG2E * jnp.exp2((-8.0 / DA_HEADS) * (h + 1).astype(F32) * jnp.ones((1, 1), F32))

    @pl.when(i == 0)
    def _():
        row = lax.broadcasted_iota(jnp.int32, (tq, tq), 0)
        col = lax.broadcasted_iota(jnp.int32, (tq, tq), 1)
        rel_scr[...] = (row - col).astype(F32) * slope

    q = (q_ref[...].astype(F32) * (LOG2E * d ** -0.5)).astype(BF16)
    qs = (q[:, :d], q[:, d:])

    m_scr[...] = jnp.full(m_scr.shape, -jnp.inf, F32)
    l_scr[...] = jnp.zeros(l_scr.shape, F32)
    acc_scr[...] = jnp.zeros(acc_scr.shape, F32)

    def block(j, masked):
        start = pl.multiple_of(j * tq, tq)
        vj = v_ref[pl.ds(start, tq), :]
        off = slope * ((i - j) * tq).astype(F32)
        for mp in range(2):
            kj = k_ref[pl.ds(start, tq), mp * d:(mp + 1) * d]
            s = lax.dot_general(qs[mp], kj, (((1,), (1,)), ((), ())), preferred_element_type=F32)
            t = s - rel_scr[...]
            if masked:
                row = lax.broadcasted_iota(jnp.int32, (tq, tq), 0)
                col = lax.broadcasted_iota(jnp.int32, (tq, tq), 1)
                t = jnp.where(row >= col, t, -jnp.inf)
            m_old = m_scr[mp]
            m_new = jnp.maximum(m_old, jnp.max(t, axis=-1, keepdims=True) - off)
            p = jnp.exp2(t - (m_new + off))
            alpha = jnp.exp2(m_old - m_new)
            l_scr[mp] = alpha * l_scr[mp] + jnp.sum(p, axis=-1, keepdims=True)
            acc_scr[mp] = alpha * acc_scr[mp] + jnp.dot(p.astype(BF16), vj, preferred_element_type=F32)
            m_scr[mp] = m_new

    def body(j, carry):
        block(j, False)
        return carry

    lax.fori_loop(0, i, body, 0)
    block(i, True)

    lam = (jnp.exp(jnp.sum(lq1_ref[...] * lk1_ref[...], axis=-1, keepdims=True))
           - jnp.exp(jnp.sum(lq2_ref[...] * lk2_ref[...], axis=-1, keepdims=True)) + lam_init)
    o = acc_scr[0] / l_scr[0] - lam * (acc_scr[1] / l_scr[1])
    ms = jnp.mean(o * o, axis=-1, keepdims=True)
    o = o * lax.rsqrt(ms + RMS_EPS) * g_ref[...] * (1.0 - lam_init)
    o_ref[...] = o.astype(o_ref.dtype)


def diff_attention(proj, batch, seq, lq1, lk1, lq2, lk2, g_subln, lam_init):
    tq = min(ATT_BLOCK, seq)
    nq = seq // tq
    hw = 2 * DA_HEAD_DIM
    kcol = DA_WIDTH // hw
    vcol = 2 * DA_WIDTH // hw
    vec = lambda a: a.reshape(1, -1).astype(F32)
    const = lambda b, h, i: (0, 0)
    return pl.pallas_call(
        functools.partial(_diff_attn_kernel, lam_init=lam_init),
        out_shape=jax.ShapeDtypeStruct((batch * seq, DA_WIDTH), BF16),
        grid=(batch, DA_HEADS, nq),
        in_specs=[pl.BlockSpec((tq, hw), lambda b, h, i: (b * nq + i, h)),
                  pl.BlockSpec((seq, hw), lambda b, h, i: (b, kcol + h)),
                  pl.BlockSpec((seq, hw), lambda b, h, i: (b, vcol + h)),
                  pl.BlockSpec((1, DA_HEAD_DIM), const),
                  pl.BlockSpec((1, DA_HEAD_DIM), const),
                  pl.BlockSpec((1, DA_HEAD_DIM), const),
                  pl.BlockSpec((1, DA_HEAD_DIM), const),
                  pl.BlockSpec((1, hw), const)],
        out_specs=pl.BlockSpec((tq, hw), lambda b, h, i: (b * nq + i, h)),
        scratch_shapes=[pltpu.VMEM((tq, tq), F32),
                        pltpu.VMEM((2, tq, 1), F32),
                        pltpu.VMEM((2, tq, 1), F32),
                        pltpu.VMEM((2, tq, hw), F32)],
        compiler_params=_params(("arbitrary", "arbitrary", "arbitrary")),
        name="diff_attention",
    )(proj, proj, proj, vec(lq1), vec(lk1), vec(lq2), vec(lk2), vec(g_subln))


GLA_STEP = 512


def _gla_kernel(q_ref, k_ref, v_ref, og_ref, ag_ref, wa_ref, ba_ref, g_ref, o_ref, state_scr):
    c = GLA_CHUNK
    n_chunks = q_ref.shape[0] // c

    @pl.when(pl.program_id(2) == 0)
    def _():
        state_scr[...] = jnp.zeros(state_scr.shape, F32)

    row = lax.broadcasted_iota(jnp.int32, (c, c), 0)
    col = lax.broadcasted_iota(jnp.int32, (c, c), 1)
    tril = row >= col
    tri_ones = tril.astype(F32)
    hi = lax.Precision.HIGHEST

    for n in range(n_chunks):
        sl = pl.ds(n * c, c)
        z = jnp.dot(ag_ref[sl, :], wa_ref[...], precision=hi, preferred_element_type=F32) + ba_ref[...]
        log_a = jax.nn.log_sigmoid(z) * (1.0 / GLA_TAU)
        b = jnp.dot(tri_ones, log_a, precision=hi, preferred_element_type=F32)
        b_mid = b[c // 2:c // 2 + 1, :]
        b_last = b[c - 1:c, :]
        q = q_ref[sl, :].astype(F32) * (GLA_DK ** -0.5)
        k = k_ref[sl, :].astype(F32)
        v = v_ref[sl, :]
        qg = (q * jnp.exp(b - b_mid)).astype(BF16)
        kg = (k * jnp.exp(b_mid - b)).astype(BF16)
        att = lax.dot_general(qg, kg, (((1,), (1,)), ((), ())), preferred_element_type=F32)
        att = jnp.where(tril, att, 0.0).astype(BF16)
        o = jnp.dot(att, v, preferred_element_type=F32)
        q_inter = (q * jnp.exp(b)).astype(BF16)
        state = state_scr[...]
        o = o + lax.dot_general(q_inter, state.astype(BF16), (((1,), (1,)), ((), ())),
                                preferred_element_type=F32)
        k_state = (k * jnp.exp(b_last - b)).astype(BF16)
        kv = lax.dot_general(v, k_state, (((0,), (0,)), ((), ())), preferred_element_type=F32)
        state_scr[...] = state * jnp.exp(b_last) + kv
        ms = jnp.mean(o * o, axis=-1, keepdims=True)
        og = og_ref[sl, :].astype(F32)
        o = o * lax.rsqrt(ms + RMS_EPS) * g_ref[...] * (og * jax.nn.sigmoid(og))
        o_ref[sl, :] = o.astype(o_ref.dtype)


def gla(proj, ag, w_alpha_up, b_alpha, g_out, batch, seq):
    ts = min(GLA_STEP, seq)
    ns = seq // ts
    qcol = 3 * DA_WIDTH // GLA_DK
    kcol = qcol + GLA_KEY_WIDTH // GLA_DK
    vcol = (3 * DA_WIDTH + 2 * GLA_KEY_WIDTH) // GLA_DV
    ocol = vcol + GLA_WIDTH // GLA_DV
    rank = ag.shape[1]
    return pl.pallas_call(
        _gla_kernel,
        out_shape=jax.ShapeDtypeStruct((batch * seq, GLA_WIDTH), BF16),
        grid=(batch, GLA_HEADS, ns),
        in_specs=[pl.BlockSpec((ts, GLA_DK), lambda b, h, i: (b * ns + i, qcol + h)),
                  pl.BlockSpec((ts, GLA_DK), lambda b, h, i: (b * ns + i, kcol + h)),
                  pl.BlockSpec((ts, GLA_DV), lambda b, h, i: (b * ns + i, vcol + h)),
                  pl.BlockSpec((ts, GLA_DV), lambda b, h, i: (b * ns + i, ocol + h)),
                  pl.BlockSpec((ts, rank), lambda b, h, i: (b * ns + i, 0)),
                  pl.BlockSpec((rank, GLA_DK), lambda b, h, i: (0, h)),
                  pl.BlockSpec((1, GLA_DK), lambda b, h, i: (0, h)),
                  pl.BlockSpec((1, GLA_DV), lambda b, h, i: (0, 0))],
        out_specs=pl.BlockSpec((ts, GLA_DV), lambda b, h, i: (b * ns + i, h)),
        scratch_shapes=[pltpu.VMEM((GLA_DV, GLA_DK), F32)],
        compiler_params=_params(("parallel", "parallel", "arbitrary")),
        name="gla",
    )(proj, proj, proj, proj, ag, w_alpha_up, b_alpha.reshape(1, -1), g_out.reshape(1, -1))


def _router_kernel(h_ref, g_ref, w_ref, b_ref, hn_ref, idx_ref, wt_ref, rank_ref, cnt_ref, cnt_scr):
    tm = h_ref.shape[0]

    @pl.when(pl.program_id(0) == 0)
    def _():
        cnt_scr[...] = jnp.zeros(cnt_scr.shape, F32)

    x = h_ref[...]
    ms = jnp.mean(x * x, axis=-1, keepdims=True)
    hn = x * lax.rsqrt(ms + RMS_EPS) * g_ref[...]
    hn_ref[...] = hn
    logits = jnp.dot(hn, w_ref[...], precision=lax.Precision.HIGHEST, preferred_element_type=F32) + b_ref[...]
    lane_i = lax.broadcasted_iota(jnp.int32, (tm, LANES), 1)
    lane = lane_i.astype(F32)
    logits = jnp.where(lane_i < N_EXPERTS, logits, -jnp.inf)

    row = lax.broadcasted_iota(jnp.int32, (tm, tm), 0)
    col = lax.broadcasted_iota(jnp.int32, (tm, tm), 1)
    strict = (row > col).astype(BF16)

    work = logits
    sel = jnp.zeros((tm, LANES), jnp.bool_)
    idx_out = jnp.zeros((tm, LANES), F32)
    val_out = jnp.full((tm, LANES), -jnp.inf, F32)
    onehots = []
    for kk in range(TOP_K):
        mx = jnp.max(work, axis=-1, keepdims=True)
        idx = jnp.min(jnp.where(work == mx, lane, float(LANES)), axis=-1, keepdims=True)
        onehot = lane == idx
        onehots.append(onehot)
        sel = jnp.logical_or(sel, onehot)
        idx_out = jnp.where(lane_i == kk, idx, idx_out)
        val_out = jnp.where(lane_i == kk, mx, val_out)
        work = jnp.where(onehot, -jnp.inf, work)

    e = jnp.exp(val_out - jnp.max(val_out, axis=-1, keepdims=True))
    wt_ref[...] = e / jnp.sum(e, axis=-1, keepdims=True)
    idx_ref[...] = idx_out.astype(jnp.int32)

    self = sel.astype(F32)
    rank_dense = jnp.dot(strict, self.astype(BF16), preferred_element_type=F32) + cnt_scr[...]
    rank_out = jnp.zeros((tm, LANES), F32)
    for kk in range(TOP_K):
        r = jnp.sum(jnp.where(onehots[kk], rank_dense, 0.0), axis=-1, keepdims=True)
        rank_out = jnp.where(lane_i == kk, r, rank_out)
    rank_ref[...] = rank_out.astype(jnp.int32)
    cnt_scr[...] = cnt_scr[...] + jnp.sum(self, axis=0, keepdims=True)
    cnt_ref[...] = cnt_scr[...].astype(jnp.int32)


def router(h, g, w_router, b_router, tm=256):
    t, d = h.shape
    wpad = jnp.zeros((d, LANES), F32).at[:, :N_EXPERTS].set(w_router)
    bpad = jnp.zeros((1, LANES), F32).at[0, :N_EXPERTS].set(b_router)
    tile = lambda i: (i, 0)
    const = lambda i: (0, 0)
    return pl.pallas_call(
        _router_kernel,
        out_shape=[jax.ShapeDtypeStruct((t, d), F32),
                   jax.ShapeDtypeStruct((t, LANES), jnp.int32),
                   jax.ShapeDtypeStruct((t, LANES), F32),
                   jax.ShapeDtypeStruct((t, LANES), jnp.int32),
                   jax.ShapeDtypeStruct((1, LANES), jnp.int32)],
        grid=(t // tm,),
        in_specs=[pl.BlockSpec((tm, d), tile), pl.BlockSpec((1, d), const),
                  pl.BlockSpec((d, LANES), const), pl.BlockSpec((1, LANES), const)],
        out_specs=[pl.BlockSpec((tm, d), tile), pl.BlockSpec((tm, LANES), tile),
                   pl.BlockSpec((tm, LANES), tile), pl.BlockSpec((tm, LANES), tile),
                   pl.BlockSpec((1, LANES), const)],
        scratch_shapes=[pltpu.VMEM((1, LANES), F32)],
        compiler_params=_params(("arbitrary",)),
        name="router",
    )(h, g.reshape(1, d), wpad, bpad)


def _expert_kernel(sbe_ref, sbc_ref, yst_ref, nv_ref, tok_ref, tokn_ref, hn_hbm, wgu_hbm, bgu_ref, sel_ref, wd_hbm,
                   bd_ref, y_hbm, x2d, stage, act, wgu_buf, wd_buf, wgu_bf, wd_bf, ybuf, zrows,
                   gsem, wgsem, wdsem, ysem, zsem, *, n1, n2):
    g = pl.program_id(0)
    n_sb = pl.num_programs(0)
    n_steps = n1 + n2
    nv = nv_ref[0]
    cnt = sbc_ref[g]
    cnt8 = pl.multiple_of(lax.shift_left(lax.shift_right_logical(cnt + 7, 3), 3), 8)
    e = sbe_ref[g]
    g_next = jnp.minimum(g + 1, n_sb - 1)
    has_next = g + 1 < nv
    cnt_next = jnp.where(has_next, sbc_ref[g_next], 0)
    e_next = sbe_ref[g_next]
    ystart = pl.multiple_of(yst_ref[g], 8)
    cur = lax.rem(g, 2)
    rb = MOE_RB
    big = MOE_BIG
    ch = stage.shape[1]
    ha = act.shape[2]
    tn1 = wgu_buf.shape[2]
    tn2 = wd_buf.shape[2]
    sems = gsem

    def chunk_rows(c, total):
        n = jnp.clip(total - c * ch, 0, ch)
        return pl.multiple_of(lax.shift_left(lax.shift_right_logical(n + 7, 3), 3), 8)

    def issue_chunk(tref, c, total, slot):
        def body(i, carry):
            for u in range(8):
                r = i * 8 + u
                pltpu.make_async_copy(hn_hbm.at[pl.ds(tref[0, 0, c * ch + r], 1), :],
                                      stage.at[slot, pl.ds(r, 1), :], sems.at[slot]).start()
            return carry
        lax.fori_loop(0, lax.shift_right_logical(chunk_rows(c, total), 3), body, 0)

    def finish_chunk(c, total, slot, xslot):
        n = chunk_rows(c, total)

        @pl.when(n > 0)
        def _():
            pltpu.make_async_copy(hn_hbm.at[pl.ds(0, n), :], stage.at[slot, pl.ds(0, n), :], sems.at[slot]).wait()

        x2d[xslot, pl.ds(pl.multiple_of(c * ch, ch), ch), :] = stage[slot].astype(BF16)

    def gather_step(st):
        @pl.when(cnt_next > 0)
        def _():
            @pl.when(st > 0)
            def _():
                finish_chunk(st - 1, cnt_next, lax.rem(st - 1, 2), 1 - cur)
            issue_chunk(tokn_ref, st, cnt_next, lax.rem(st, 2))

    def gu_copy(ex, j, slot):
        return pltpu.make_async_copy(wgu_hbm.at[ex, :, pl.ds(pl.multiple_of(j * tn1, tn1), tn1)],
                                     wgu_buf.at[slot], wgsem.at[slot])

    def d_copy(ex, j, slot):
        return pltpu.make_async_copy(wd_hbm.at[ex, :, pl.ds(pl.multiple_of(j * tn2, tn2), tn2)],
                                     wd_buf.at[slot], wdsem.at[slot])

    def y_piece(slot, j, r0, size):
        return pltpu.make_async_copy(
            ybuf.at[slot, pl.ds(r0, size), :],
            y_hbm.at[pl.ds(pl.multiple_of(ystart + r0, 8), size), pl.ds(pl.multiple_of(j * tn2, tn2), tn2)],
            ysem.at[slot])

    def y_issue(slot, j):
        n_a = lax.shift_right_logical(cnt8, 9)
        rem_a = cnt8 - n_a * 512
        n_b = lax.shift_right_logical(rem_a, 7)
        n_c = lax.shift_right_logical(rem_a - n_b * 128, 3)

        def a_body(i, carry):
            y_piece(slot, j, pl.multiple_of(i * 512, 512), 512).start()
            return carry

        def b_body(i, carry):
            y_piece(slot, j, pl.multiple_of(n_a * 512 + i * 128, 128), 128).start()
            return carry

        def c_body(i, carry):
            y_piece(slot, j, pl.multiple_of(n_a * 512 + n_b * 128 + i * 8, 8), 8).start()
            return carry

        lax.fori_loop(0, n_a, a_body, 0)
        lax.fori_loop(0, n_b, b_body, 0)
        lax.fori_loop(0, n_c, c_body, 0)

    def y_wait(slot):
        @pl.when(cnt8 > 0)
        def _():
            pltpu.make_async_copy(ybuf.at[slot, pl.ds(0, cnt8), :], y_hbm.at[pl.ds(0, cnt8), pl.ds(0, tn2)],
                                  ysem.at[slot]).wait()

    n_big = lax.shift_right_logical(cnt, big.bit_length() - 1)
    n_small = lax.shift_right_logical(cnt - n_big * big + (rb - 1), rb.bit_length() - 1)

    def for_row_blocks(fn):
        def big_body(i, carry):
            fn(pl.multiple_of(i * big, big), big)
            return carry

        def small_body(i, carry):
            fn(pl.multiple_of(n_big * big + i * rb, rb), rb)
            return carry

        lax.fori_loop(0, n_big, big_body, 0)
        lax.fori_loop(0, n_small, small_body, 0)

    def gu_step(s, carry):
        slot = lax.rem(s, 2)
        gu_copy(e, s, slot).wait()

        @pl.when(s + 1 < n1)
        def _():
            gu_copy(e, s + 1, 1 - slot).start()

        @pl.when(s + 1 == n1)
        def _():
            d_copy(e, 0, 0).start()

        gather_step(s)
        wgu_bf[...] = wgu_buf[slot].astype(BF16)
        bias = bgu_ref[0, pl.ds(s, 1), :]

        def sub(start, size):
            rows = pl.ds(start, size)
            gu = jnp.dot(x2d[cur, rows, :], wgu_bf[...], preferred_element_type=F32) + bias
            nxt = pltpu.roll(gu, tn1 - 1, 1)
            gate = jnp.minimum(gu, SWIGLU_LIMIT)
            up = jnp.clip(nxt, -SWIGLU_LIMIT, SWIGLU_LIMIT)
            a = gate * jax.nn.sigmoid(gate * SWIGLU_ALPHA) * (up + 1.0)
            act[s, rows, :] = jnp.dot(a.astype(BF16), sel_ref[...], preferred_element_type=F32).astype(BF16)

        for_row_blocks(sub)
        return carry

    def d_step(j, carry):
        slot = lax.rem(j, 2)
        d_copy(e, j, slot).wait()

        @pl.when(j + 1 < n2)
        def _():
            d_copy(e, j + 1, 1 - slot).start()

        @pl.when(jnp.logical_and(j + 1 == n2, has_next))
        def _():
            gu_copy(e_next, 0, 0).start()

        gather_step(n1 + j)
        wd_bf[...] = wd_buf[slot].astype(BF16)
        bias = bd_ref[0, pl.ds(j, 1), :]

        @pl.when(j >= 2)
        def _():
            y_wait(slot)

        def sub(start, size):
            rows = pl.ds(start, size)
            y = bias
            for jj in range(0, n1, 2):
                a2 = jnp.concatenate([act[jj, rows, :], act[jj + 1, rows, :]], axis=1)
                y = y + jnp.dot(a2, wd_bf[jj * ha:(jj + 2) * ha, :], preferred_element_type=F32)
            ybuf[slot, rows, :] = y

        for_row_blocks(sub)
        y_issue(slot, j)
        return carry

    @pl.when(g < nv)
    def _():
        @pl.when(g == 0)
        def _():
            gu_copy(e, 0, 0).start()
            stage[...] = jnp.zeros(stage.shape, stage.dtype)

            def body(c, carry):
                issue_chunk(tok_ref, c, cnt, 0)
                finish_chunk(c, cnt, 0, 0)
                return carry
            lax.fori_loop(0, n_steps, body, 0)

        @pl.when(g > 0)
        def _():
            finish_chunk(n_steps - 1, cnt, (n_steps - 1) % 2, cur)

        lax.fori_loop(0, n1, gu_step, 0)
        lax.fori_loop(0, n2, d_step, 0)
        y_wait(0)
        y_wait(1)

        @pl.when(g == nv - 1)
        def _():
            zrows[...] = jnp.zeros(zrows.shape, zrows.dtype)
            first = ystart + cnt8
            n_tail = lax.shift_right_logical(y_hbm.shape[0] - first, 3)

            def tail_copy(i):
                return pltpu.make_async_copy(zrows, y_hbm.at[pl.ds(pl.multiple_of(first + i * 8, 8), 8), :], zsem)

            def start(i, carry):
                tail_copy(i).start()
                return carry

            def wait(i, carry):
                tail_copy(i).wait()
                return carry

            lax.fori_loop(0, n_tail, start, 0)
            lax.fori_loop(0, n_tail, wait, 0)


def expert_ffn(hn, tok_table, sb_expert, sb_count, sb_ystart, n_valid, n_out_rows, w_gate_up, b_gate_up, w_down,
               b_down):
    n_e, d, f2 = w_gate_up.shape
    f = f2 // 2
    r = MOE_SB
    n_sb = sb_expert.shape[0]
    n1, n2 = f2 // GU_TN, d // DN_TN
    assert n1 % 2 == 0 and n2 >= 2 and r % (n1 + n2) == 0
    ch = r // (n1 + n2)
    ha = GU_TN // 2
    sel = (jnp.arange(GU_TN)[:, None] == 2 * jnp.arange(ha)[None, :]).astype(BF16)

    def cur(g, nv):
        return jnp.maximum(jnp.minimum(g, nv[0] - 1), 0)

    def nxt(g, nv):
        return jnp.maximum(jnp.minimum(g + 1, nv[0] - 1), 0)

    const = lambda g, sbe, sbc, yst, nv: (0, 0)
    grid_spec = pltpu.PrefetchScalarGridSpec(
        num_scalar_prefetch=4,
        grid=(n_sb,),
        in_specs=[pl.BlockSpec((1, 1, r), lambda g, sbe, sbc, yst, nv: (cur(g, nv), 0, 0), memory_space=pltpu.SMEM),
                  pl.BlockSpec((1, 1, r), lambda g, sbe, sbc, yst, nv: (nxt(g, nv), 0, 0), memory_space=pltpu.SMEM),
                  pl.BlockSpec(memory_space=pl.ANY),
                  pl.BlockSpec(memory_space=pl.ANY),
                  pl.BlockSpec((1, n1, GU_TN), lambda g, sbe, sbc, yst, nv: (sbe[cur(g, nv)], 0, 0)),
                  pl.BlockSpec((GU_TN, ha), const),
                  pl.BlockSpec(memory_space=pl.ANY),
                  pl.BlockSpec((1, n2, DN_TN), lambda g, sbe, sbc, yst, nv: (sbe[cur(g, nv)], 0, 0))],
        out_specs=pl.BlockSpec(memory_space=pl.ANY),
        scratch_shapes=[pltpu.VMEM((2, r, d), BF16),
                        pltpu.VMEM((2, ch, d), F32),
                        pltpu.VMEM((n1, r, ha), BF16),
                        pltpu.VMEM((2, d, GU_TN), F32),
                        pltpu.VMEM((2, f, DN_TN), F32),
                        pltpu.VMEM((d, GU_TN), BF16),
                        pltpu.VMEM((f, DN_TN), BF16),
                        pltpu.VMEM((2, r, DN_TN), F32),
                        pltpu.VMEM((8, d), F32),
                        pltpu.SemaphoreType.DMA((2,)), pltpu.SemaphoreType.DMA((2,)),
                        pltpu.SemaphoreType.DMA((2,)), pltpu.SemaphoreType.DMA((2,)),
                        pltpu.SemaphoreType.DMA(())],
    )
    tok3 = tok_table.reshape(n_sb, 1, r)
    return pl.pallas_call(
        functools.partial(_expert_kernel, n1=n1, n2=n2),
        out_shape=jax.ShapeDtypeStruct((n_out_rows, d), F32),
        grid_spec=grid_spec,
        compiler_params=_params(("arbitrary",), vmem=MOE_VMEM_LIMIT),
        name="expert_ffn",
    )(sb_expert, sb_count, sb_ystart, n_valid, tok3, tok3, hn, w_gate_up, b_gate_up.reshape(n_e, n1, GU_TN), sel,
      w_down, b_down.reshape(n_e, n2, DN_TN))


def _combine_kernel(pos_ref, posn_ref, y_hbm, h_ref, wt_ref, g_ref, h2_ref, hn_ref, buf, sems):
    tm = h_ref.shape[0]
    i = pl.program_id(0)
    slot = lax.rem(i, 2)

    def gather(pref, sl, start):
        def body(s, carry):
            r = lax.shift_right_logical(s, TOP_K.bit_length() - 1)
            kk = lax.rem(s, TOP_K)
            cp = pltpu.make_async_copy(y_hbm.at[pl.ds(pref[0, 0, s], 1), :], buf.at[sl, kk, pl.ds(r, 1), :],
                                       sems.at[sl])
            if start:
                cp.start()
            else:
                cp.wait()
            return carry
        lax.fori_loop(0, tm * TOP_K, body, 0)

    @pl.when(i == 0)
    def _():
        gather(pos_ref, slot, True)

    @pl.when(i + 1 < pl.num_programs(0))
    def _():
        gather(posn_ref, 1 - slot, True)

    gather(pos_ref, slot, False)
    acc = h_ref[...]
    wt = wt_ref[...]
    for kk in range(TOP_K):
        acc = acc + wt[:, kk:kk + 1] * buf[slot, kk]
    h2_ref[...] = acc
    ms = jnp.mean(acc * acc, axis=-1, keepdims=True)
    hn_ref[...] = (acc * lax.rsqrt(ms + RMS_EPS) * g_ref[...]).astype(hn_ref.dtype)


def combine(y, pos_flat, h, wt, g_next, tm=128):
    t, d = h.shape
    nt = t // tm
    tile = lambda i: (i, 0)
    pos3 = pos_flat.reshape(nt, 1, tm * TOP_K)
    return pl.pallas_call(
        _combine_kernel,
        out_shape=[jax.ShapeDtypeStruct((t, d), F32), jax.ShapeDtypeStruct((t, d), BF16)],
        grid=(nt,),
        in_specs=[pl.BlockSpec((1, 1, tm * TOP_K), lambda i: (i, 0, 0), memory_space=pltpu.SMEM),
                  pl.BlockSpec((1, 1, tm * TOP_K), lambda i: (jnp.minimum(i + 1, nt - 1), 0, 0),
                               memory_space=pltpu.SMEM),
                  pl.BlockSpec(memory_space=pl.ANY),
                  pl.BlockSpec((tm, d), tile),
                  pl.BlockSpec((tm, LANES), tile),
                  pl.BlockSpec((1, d), lambda i: (0, 0))],
        out_specs=[pl.BlockSpec((tm, d), tile), pl.BlockSpec((tm, d), tile)],
        scratch_shapes=[pltpu.VMEM((2, TOP_K, tm, d), F32), pltpu.SemaphoreType.DMA((2,))],
        compiler_params=_params(("arbitrary",)),
        name="moe_combine",
    )(pos3, pos3, y, h, wt, g_next.reshape(1, d))


def _moe_layout(idx, rank, counts):
    t = idx.shape[0]
    r = MOE_SB
    n_sb = N_EXPERTS + -(-(t * TOP_K) // r)
    nsb = (counts + r - 1) // r
    sb_end = jnp.cumsum(nsb)
    sb_first = sb_end - nsb
    n_valid = sb_end[-1:]
    tok_pos = sb_first[idx] * r + rank
    tok = jnp.broadcast_to(jnp.arange(t, dtype=jnp.int32)[:, None], (t, TOP_K))
    tok_table = jnp.zeros((n_sb * r,), jnp.int32).at[tok_pos.reshape(-1)].set(tok.reshape(-1))
    rows8 = (counts + 7) // 8 * 8
    y_first = jnp.cumsum(rows8) - rows8
    pos = y_first[idx] + rank
    g = jnp.arange(n_sb, dtype=jnp.int32)
    gv = jnp.minimum(g, n_valid - 1)
    sb_expert = jnp.sum(gv[:, None] >= sb_end[None, :], axis=1).astype(jnp.int32)
    sb_local = gv - sb_first[sb_expert]
    sb_count = jnp.clip(counts[sb_expert] - sb_local * r, 0, r)
    sb_count = jnp.where(g < n_valid, sb_count, 0)
    sb_ystart = y_first[sb_expert] + sb_local * r
    n_out_rows = t * TOP_K + N_EXPERTS * 8
    i32 = lambda a: a.astype(jnp.int32)
    return i32(pos), tok_table, i32(sb_expert), i32(sb_count), i32(sb_ystart), i32(n_valid), n_out_rows


def _layer(h, p, lam_init, layer, norm_mix, w_in_all, w_alpha_up, b_alpha, lq1, lk1, lq2, lk2, g_diff_subln,
           g_gla_out, w_branch, w_merge_gate, b_merge_gate, w_out, norm_moe, w_router, b_router, w_gate_up,
           b_gate_up, w_down, b_down, norm_ple, w_ple_proj, w_ple_gate, batch, seq):
    t, d = h.shape
    xn = rmsnorm(h, norm_mix, BF16)
    proj = matmul_wcast(xn, w_in_all, layer, BF16, n_cols=N_PROJ)
    w_ag = jnp.zeros((d, LANES), BF16).at[:, :GLA_GATE_RANK].set(w_in_all[layer, :, N_PROJ:].astype(BF16))
    ag = matmul(xn, w_ag, F32)
    w_au = jnp.zeros((LANES, GLA_KEY_WIDTH), F32).at[:GLA_GATE_RANK].set(w_alpha_up)
    gates = matmul_wcast_bias_sigmoid(xn, w_merge_gate, b_merge_gate, BF16)
    o_a = diff_attention(proj, batch, seq, lq1, lk1, lq2, lk2, g_diff_subln, lam_init)
    o_b = gla(proj, ag, w_au, b_alpha, g_gla_out, batch, seq)
    merged = merge_branches(o_a, o_b, w_branch, gates)
    h = matmul_residual(merged, w_out, h)
    hn, idx, wt, rank, counts = router(h, norm_moe, w_router, b_router)
    pos, tok_table, sb_expert, sb_count, sb_ystart, n_valid, n_out_rows = _moe_layout(
        idx[:, :TOP_K], rank[:, :TOP_K], counts[0, :N_EXPERTS])
    ys = expert_ffn(hn, tok_table, sb_expert, sb_count, sb_ystart, n_valid, n_out_rows, w_gate_up, b_gate_up,
                    w_down, b_down)
    h, hn = combine(ys, pos.reshape(-1), h, wt, norm_ple)
    h = ple_update(hn, w_ple_gate, p.astype(BF16), w_ple_proj.astype(BF16), h)
    return h


def kernel(x, p, norm_mix, w_in, w_alpha_up, b_alpha, lambda_q1, lambda_k1, lambda_q2, lambda_k2, g_diff_subln, g_gla_out, w_branch, w_merge_gate, b_merge_gate, w_out, norm_moe, w_router, b_router, w_gate_up, b_gate_up, w_down, b_down, norm_ple, w_ple_proj, w_ple_gate, norm_final):
    batch, seq, d = x.shape
    depth = w_in.shape[0]
    h = x.reshape(batch * seq, d)
    for i in range(depth):
        lam_init = 0.8 - 0.6 * math.exp(-0.3 * i)
        h = _layer(h, p[i].reshape(batch * seq, -1), lam_init, i, norm_mix[i], w_in, w_alpha_up[i], b_alpha[i],
                   lambda_q1[i], lambda_k1[i], lambda_q2[i], lambda_k2[i], g_diff_subln[i], g_gla_out[i],
                   w_branch[i], w_merge_gate[i], b_merge_gate[i], w_out[i], norm_moe[i], w_router[i],
                   b_router[i], w_gate_up[i], b_gate_up[i], w_down[i], b_down[i], norm_ple[i], w_ple_proj[i],
                   w_ple_gate[i], batch, seq)
    out = rmsnorm(h, norm_final, F32)
    return out.reshape(batch, seq, d)
```

```python
import functools
import math

import jax
import jax.numpy as jnp
from jax import lax
from jax.experimental import pallas as pl
from jax.experimental.pallas import tpu as pltpu

F32 = jnp.float32
BF16 = jnp.bfloat16

RMS_EPS = 1e-6
DA_HEADS = 8
DA_HEAD_DIM = 128
DA_WIDTH = DA_HEADS * 2 * DA_HEAD_DIM
GLA_HEADS = 4
GLA_DK = 256
GLA_DV = 512
GLA_KEY_WIDTH = GLA_HEADS * GLA_DK
GLA_WIDTH = GLA_HEADS * GLA_DV
GLA_GATE_RANK = 16
GLA_TAU = 16.0
GLA_CHUNK = 64
N_PROJ = 3 * DA_WIDTH + 2 * GLA_KEY_WIDTH + 2 * GLA_WIDTH
N_EXPERTS = 32
TOP_K = 4
SWIGLU_LIMIT = 7.0
SWIGLU_ALPHA = 1.702
LANES = 128
MOE_SB = 1280
MOE_BIG = 512
MOE_RB = 128
MOE_CH_GU = 96
MOE_CH_DN = 16
GU_TN = 256
DN_TN = 512

VMEM_LIMIT = 56 * 1024 * 1024
MOE_VMEM_LIMIT = 60 * 1024 * 1024


def _params(sem, vmem=VMEM_LIMIT):
    return pltpu.CompilerParams(dimension_semantics=sem, vmem_limit_bytes=vmem)


def _rmsnorm_kernel(x_ref, g_ref, o_ref):
    x = x_ref[...]
    ms = jnp.mean(x * x, axis=-1, keepdims=True)
    o_ref[...] = (x * lax.rsqrt(ms + RMS_EPS) * g_ref[...]).astype(o_ref.dtype)


def rmsnorm(x, g, out_dtype, tm=256):
    t, d = x.shape
    return pl.pallas_call(
        _rmsnorm_kernel,
        out_shape=jax.ShapeDtypeStruct((t, d), out_dtype),
        grid=(t // tm,),
        in_specs=[pl.BlockSpec((tm, d), lambda i: (i, 0)),
                  pl.BlockSpec((1, d), lambda i: (0, 0))],
        out_specs=pl.BlockSpec((tm, d), lambda i: (i, 0)),
        compiler_params=_params(("parallel",)),
        name="rmsnorm",
    )(x, g.reshape(1, d))


def _mm_kernel(x_ref, w_ref, o_ref):
    o_ref[...] = jnp.dot(x_ref[...], w_ref[...], preferred_element_type=F32).astype(o_ref.dtype)


def matmul(x, w, out_dtype, tm=1024, tn=1024, n_cols=None):
    m, k = x.shape
    n = w.shape[1] if n_cols is None else n_cols
    tm, tn = min(tm, m), min(tn, n)
    return pl.pallas_call(
        _mm_kernel,
        out_shape=jax.ShapeDtypeStruct((m, n), out_dtype),
        grid=(m // tm, n // tn),
        in_specs=[pl.BlockSpec((tm, k), lambda i, j: (i, 0)),
                  pl.BlockSpec((k, tn), lambda i, j: (0, j))],
        out_specs=pl.BlockSpec((tm, tn), lambda i, j: (i, j)),
        compiler_params=_params(("parallel", "parallel")),
        name="matmul",
    )(x, w)


def _mm_wcast_kernel(x_ref, w_ref, o_ref, wbf):
    @pl.when(pl.program_id(1) == 0)
    def _():
        wbf[...] = w_ref[...].astype(BF16)

    o_ref[...] = jnp.dot(x_ref[...], wbf[...], preferred_element_type=F32).astype(o_ref.dtype)


def matmul_wcast(x, w, layer, out_dtype, n_cols=None, tm=1024, tn=512):
    m, k = x.shape
    n = w.shape[2] if n_cols is None else n_cols
    return pl.pallas_call(
        _mm_wcast_kernel,
        out_shape=jax.ShapeDtypeStruct((m, n), out_dtype),
        grid=(n // tn, m // tm),
        in_specs=[pl.BlockSpec((tm, k), lambda j, i: (i, 0)),
                  pl.BlockSpec((None, k, tn), lambda j, i: (layer, 0, j))],
        out_specs=pl.BlockSpec((tm, tn), lambda j, i: (i, j)),
        scratch_shapes=[pltpu.VMEM((k, tn), BF16)],
        compiler_params=_params(("arbitrary", "arbitrary")),
        name="matmul_wcast",
    )(x, w)


def _mm_wcast_bias_sigmoid_kernel(x_ref, w_ref, b_ref, o_ref, wbf):
    @pl.when(pl.program_id(1) == 0)
    def _():
        wbf[...] = w_ref[...].astype(BF16)

    z = jnp.dot(x_ref[...], wbf[...], preferred_element_type=F32) + b_ref[...]
    o_ref[...] = jax.nn.sigmoid(z).astype(o_ref.dtype)


def matmul_wcast_bias_sigmoid(x, w, b, out_dtype, tm=1024, tn=512):
    m, k = x.shape
    n = w.shape[1]
    return pl.pallas_call(
        _mm_wcast_bias_sigmoid_kernel,
        out_shape=jax.ShapeDtypeStruct((m, n), out_dtype),
        grid=(n // tn, m // tm),
        in_specs=[pl.BlockSpec((tm, k), lambda j, i: (i, 0)),
                  pl.BlockSpec((k, tn), lambda j, i: (0, j)),
                  pl.BlockSpec((1, tn), lambda j, i: (0, j))],
        out_specs=pl.BlockSpec((tm, tn), lambda j, i: (i, j)),
        scratch_shapes=[pltpu.VMEM((k, tn), BF16)],
        compiler_params=_params(("arbitrary", "arbitrary")),
        name="matmul_wcast_bias_sigmoid",
    )(x, w, b.reshape(1, n))


def _mm_residual_kernel(x_ref, w_ref, r_ref, o_ref, wbf):
    @pl.when(pl.program_id(1) == 0)
    def _():
        wbf[...] = w_ref[...].astype(BF16)

    o_ref[...] = r_ref[...] + jnp.dot(x_ref[...], wbf[...], preferred_element_type=F32)


def matmul_residual(x, w, r, tm=1024, tn=512):
    m, k = x.shape
    n = w.shape[1]
    return pl.pallas_call(
        _mm_residual_kernel,
        out_shape=jax.ShapeDtypeStruct((m, n), F32),
        grid=(n // tn, m // tm),
        in_specs=[pl.BlockSpec((tm, k), lambda j, i: (i, 0)),
                  pl.BlockSpec((k, tn), lambda j, i: (0, j)),
                  pl.BlockSpec((tm, tn), lambda j, i: (i, j))],
        out_specs=pl.BlockSpec((tm, tn), lambda j, i: (i, j)),
        scratch_shapes=[pltpu.VMEM((k, tn), BF16)],
        compiler_params=_params(("arbitrary", "arbitrary")),
        name="matmul_residual",
    )(x, w, r)


def _merge_kernel(oa_ref, ob_ref, wa_ref, wb_ref, ga_ref, gb_ref, o_ref, wa_bf, wb_bf):
    @pl.when(pl.program_id(1) == 0)
    def _():
        wa_bf[...] = wa_ref[...].astype(BF16)
        wb_bf[...] = wb_ref[...].astype(BF16)

    ya = jnp.dot(oa_ref[...], wa_bf[...], preferred_element_type=F32)
    yb = jnp.dot(ob_ref[...], wb_bf[...], preferred_element_type=F32)
    o_ref[...] = (ga_ref[...].astype(F32) * ya + gb_ref[...].astype(F32) * yb).astype(o_ref.dtype)


def merge_branches(o_a, o_b, w_branch, gates, tm=1024, tn=512):
    m, ka = o_a.shape
    kb = o_b.shape[1]
    n = w_branch.shape[1]
    nb = n // tn
    return pl.pallas_call(
        _merge_kernel,
        out_shape=jax.ShapeDtypeStruct((m, n), BF16),
        grid=(nb, m // tm),
        in_specs=[pl.BlockSpec((tm, ka), lambda j, i: (i, 0)),
                  pl.BlockSpec((tm, kb), lambda j, i: (i, 0)),
                  pl.BlockSpec((ka, tn), lambda j, i: (0, j)),
                  pl.BlockSpec((kb, tn), lambda j, i: (1, j)),
                  pl.BlockSpec((tm, tn), lambda j, i: (i, j)),
                  pl.BlockSpec((tm, tn), lambda j, i, nb=nb: (i, j + nb))],
        out_specs=pl.BlockSpec((tm, tn), lambda j, i: (i, j)),
        scratch_shapes=[pltpu.VMEM((ka, tn), BF16), pltpu.VMEM((kb, tn), BF16)],
        compiler_params=_params(("arbitrary", "arbitrary")),
        name="merge_branches",
    )(o_a, o_b, w_branch, w_branch, gates, gates)


def _ple_kernel(hn_ref, wg_ref, p_ref, wp_ref, h_ref, o_ref, wg_bf):
    @pl.when(pl.program_id(1) == 0)
    def _():
        wg_bf[...] = wg_ref[...].astype(BF16)

    g = jax.nn.sigmoid(jnp.dot(hn_ref[...], wg_bf[...], preferred_element_type=F32))
    pp = jnp.dot(p_ref[...], wp_ref[...], preferred_element_type=F32)
    o_ref[...] = h_ref[...] + g * pp


def ple_update(hn, w_gate, p, w_proj, h, tm=1024, tn=512):
    m, k = hn.shape
    kp = p.shape[1]
    n = w_gate.shape[1]
    return pl.pallas_call(
        _ple_kernel,
        out_shape=jax.ShapeDtypeStruct((m, n), F32),
        grid=(n // tn, m // tm),
        in_specs=[pl.BlockSpec((tm, k), lambda j, i: (i, 0)),
                  pl.BlockSpec((k, tn), lambda j, i: (0, j)),
                  pl.BlockSpec((tm, kp), lambda j, i: (i, 0)),
                  pl.BlockSpec((kp, tn), lambda j, i: (0, j)),
                  pl.BlockSpec((tm, tn), lambda j, i: (i, j))],
        out_specs=pl.BlockSpec((tm, tn), lambda j, i: (i, j)),
        scratch_shapes=[pltpu.VMEM((k, tn), BF16)],
        compiler_params=_params(("arbitrary", "arbitrary")),
        name="ple_update",
    )(hn, w_gate, p, w_proj, h)


ATT_BLOCK = 512
ATT_HEADS = 1

LOG2E = 1.4426950408889634


def _diff_attn_kernel(q_ref, k_ref, v_ref, lq1_ref, lk1_ref, lq2_ref, lk2_ref, g_ref, o_ref,
                      rel_scr, m_scr, l_scr, acc_scr, *, lam_init):
    tq = q_ref.shape[0]
    d = DA_HEAD_DIM
    hw = 2 * d
    hp = ATT_HEADS
    i = pl.program_id(2)
    slopes = [LOG2E * jnp.exp2((-8.0 / DA_HEADS) * (pl.program_id(1) * hp + hh + 1).astype(F32)
                               * jnp.ones((1, 1), F32)) for hh in range(hp)]

    @pl.when(i == 0)
    def _():
        row = lax.broadcasted_iota(jnp.int32, (tq, tq), 0)
        col = lax.broadcasted_iota(jnp.int32, (tq, tq), 1)
        for hh in range(hp):
            rel_scr[hh] = (row - col).astype(F32) * slopes[hh]

    q = (q_ref[...].astype(F32) * (LOG2E * d ** -0.5)).astype(BF16)

    m_scr[...] = jnp.full(m_scr.shape, -jnp.inf, F32)
    l_scr[...] = jnp.zeros(l_scr.shape, F32)
    acc_scr[...] = jnp.zeros(acc_scr.shape, F32)

    def block(j, masked):
        start = pl.multiple_of(j * tq, tq)
        state = [(m_scr[c], l_scr[c], acc_scr[c]) for c in range(2 * hp)]
        new_state = []
        for hh in range(hp):
            vj = v_ref[pl.ds(start, tq), hh * hw:(hh + 1) * hw]
            off = slopes[hh] * ((i - j) * tq).astype(F32)
            for mp in range(2):
                c = 2 * hh + mp
                kj = k_ref[pl.ds(start, tq), c * d:(c + 1) * d]
                s = lax.dot_general(q[:, c * d:(c + 1) * d], kj, (((1,), (1,)), ((), ())),
                                    preferred_element_type=F32)
                t = s - rel_scr[hh]
                if masked:
                    row = lax.broadcasted_iota(jnp.int32, (tq, tq), 0)
                    col = lax.broadcasted_iota(jnp.int32, (tq, tq), 1)
                    t = jnp.where(row >= col, t, -jnp.inf)
                m_old, l_old, acc_old = state[c]
                m_new = jnp.maximum(m_old, jnp.max(t, axis=-1, keepdims=True) - off)
                p = jnp.exp2(t - (m_new + off))
                alpha = jnp.exp2(m_old - m_new)
                l_new = alpha * l_old + jnp.sum(p, axis=-1, keepdims=True)
                acc_new = alpha * acc_old + jnp.dot(p.astype(BF16), vj, preferred_element_type=F32)
                new_state.append((m_new, l_new, acc_new))
        for c in range(2 * hp):
            m_scr[c], l_scr[c], acc_scr[c] = new_state[c]

    def body(j, carry):
        block(j, False)
        return carry

    lax.fori_loop(0, i, body, 0)
    block(i, True)

    lam = (jnp.exp(jnp.sum(lq1_ref[...] * lk1_ref[...], axis=-1, keepdims=True))
           - jnp.exp(jnp.sum(lq2_ref[...] * lk2_ref[...], axis=-1, keepdims=True)) + lam_init)
    for hh in range(hp):
        o = acc_scr[2 * hh] / l_scr[2 * hh] - lam * (acc_scr[2 * hh + 1] / l_scr[2 * hh + 1])
        ms = jnp.mean(o * o, axis=-1, keepdims=True)
        o = o * lax.rsqrt(ms + RMS_EPS) * g_ref[...] * (1.0 - lam_init)
        o_ref[:, hh * hw:(hh + 1) * hw] = o.astype(o_ref.dtype)


def diff_attention(proj, batch, seq, lq1, lk1, lq2, lk2, g_subln, lam_init):
    tq = min(ATT_BLOCK, seq)
    nq = seq // tq
    hp = ATT_HEADS
    hw = 2 * DA_HEAD_DIM
    gw = hp * hw
    kcol = DA_WIDTH // gw
    vcol = 2 * DA_WIDTH // gw
    vec = lambda a: a.reshape(1, -1).astype(F32)
    const = lambda b, h, i: (0, 0)
    return pl.pallas_call(
        functools.partial(_diff_attn_kernel, lam_init=lam_init),
        out_shape=jax.ShapeDtypeStruct((batch * seq, DA_WIDTH), BF16),
        grid=(batch, DA_HEADS // hp, nq),
        in_specs=[pl.BlockSpec((tq, gw), lambda b, h, i: (b * nq + i, h)),
                  pl.BlockSpec((seq, gw), lambda b, h, i: (b, kcol + h)),
                  pl.BlockSpec((seq, gw), lambda b, h, i: (b, vcol + h)),
                  pl.BlockSpec((1, DA_HEAD_DIM), const),
                  pl.BlockSpec((1, DA_HEAD_DIM), const),
                  pl.BlockSpec((1, DA_HEAD_DIM), const),
                  pl.BlockSpec((1, DA_HEAD_DIM), const),
                  pl.BlockSpec((1, hw), const)],
        out_specs=pl.BlockSpec((tq, gw), lambda b, h, i: (b * nq + i, h)),
        scratch_shapes=[pltpu.VMEM((hp, tq, tq), F32),
                        pltpu.VMEM((2 * hp, tq, 1), F32),
                        pltpu.VMEM((2 * hp, tq, 1), F32),
                        pltpu.VMEM((2 * hp, tq, hw), F32)],
        compiler_params=_params(("arbitrary", "arbitrary", "arbitrary")),
        name="diff_attention",
    )(proj, proj, proj, vec(lq1), vec(lk1), vec(lq2), vec(lk2), vec(g_subln))


GLA_STEP = 512


def _gla_kernel(q_ref, k_ref, v_ref, og_ref, ag_ref, wa_ref, ba_ref, g_ref, o_ref, state_scr):
    c = GLA_CHUNK
    n_chunks = q_ref.shape[0] // c

    @pl.when(pl.program_id(2) == 0)
    def _():
        state_scr[...] = jnp.zeros(state_scr.shape, F32)

    row = lax.broadcasted_iota(jnp.int32, (c, c), 0)
    col = lax.broadcasted_iota(jnp.int32, (c, c), 1)
    tril = row >= col
    tri_ones = tril.astype(F32)
    hi = lax.Precision.HIGHEST

    for n in range(n_chunks):
        sl = pl.ds(n * c, c)
        z = jnp.dot(ag_ref[sl, :], wa_ref[...], precision=hi, preferred_element_type=F32) + ba_ref[...]
        log_a = jax.nn.log_sigmoid(z) * (1.0 / GLA_TAU)
        b = jnp.dot(tri_ones, log_a, precision=hi, preferred_element_type=F32)
        b_mid = b[c // 2:c // 2 + 1, :]
        b_last = b[c - 1:c, :]
        q = q_ref[sl, :].astype(F32) * (GLA_DK ** -0.5)
        k = k_ref[sl, :].astype(F32)
        v = v_ref[sl, :]
        qg = (q * jnp.exp(b - b_mid)).astype(BF16)
        kg = (k * jnp.exp(b_mid - b)).astype(BF16)
        att = lax.dot_general(qg, kg, (((1,), (1,)), ((), ())), preferred_element_type=F32)
        att = jnp.where(tril, att, 0.0).astype(BF16)
        o = jnp.dot(att, v, preferred_element_type=F32)
        q_inter = (q * jnp.exp(b)).astype(BF16)
        state = state_scr[...]
        o = o + lax.dot_general(q_inter, state.astype(BF16), (((1,), (1,)), ((), ())),
                                preferred_element_type=F32)
        k_state = (k * jnp.exp(b_last - b)).astype(BF16)
        kv = lax.dot_general(v, k_state, (((0,), (0,)), ((), ())), preferred_element_type=F32)
        state_scr[...] = state * jnp.exp(b_last) + kv
        ms = jnp.mean(o * o, axis=-1, keepdims=True)
        og = og_ref[sl, :].astype(F32)
        o = o * lax.rsqrt(ms + RMS_EPS) * g_ref[...] * (og * jax.nn.sigmoid(og))
        o_ref[sl, :] = o.astype(o_ref.dtype)


def gla(proj, ag, w_alpha_up, b_alpha, g_out, batch, seq):
    ts = min(GLA_STEP, seq)
    ns = seq // ts
    qcol = 3 * DA_WIDTH // GLA_DK
    kcol = qcol + GLA_KEY_WIDTH // GLA_DK
    vcol = (3 * DA_WIDTH + 2 * GLA_KEY_WIDTH) // GLA_DV
    ocol = vcol + GLA_WIDTH // GLA_DV
    rank = ag.shape[1]
    return pl.pallas_call(
        _gla_kernel,
        out_shape=jax.ShapeDtypeStruct((batch * seq, GLA_WIDTH), BF16),
        grid=(batch, GLA_HEADS, ns),
        in_specs=[pl.BlockSpec((ts, GLA_DK), lambda b, h, i: (b * ns + i, qcol + h)),
                  pl.BlockSpec((ts, GLA_DK), lambda b, h, i: (b * ns + i, kcol + h)),
                  pl.BlockSpec((ts, GLA_DV), lambda b, h, i: (b * ns + i, vcol + h)),
                  pl.BlockSpec((ts, GLA_DV), lambda b, h, i: (b * ns + i, ocol + h)),
                  pl.BlockSpec((ts, rank), lambda b, h, i: (b * ns + i, 0)),
                  pl.BlockSpec((rank, GLA_DK), lambda b, h, i: (0, h)),
                  pl.BlockSpec((1, GLA_DK), lambda b, h, i: (0, h)),
                  pl.BlockSpec((1, GLA_DV), lambda b, h, i: (0, 0))],
        out_specs=pl.BlockSpec((ts, GLA_DV), lambda b, h, i: (b * ns + i, h)),
        scratch_shapes=[pltpu.VMEM((GLA_DV, GLA_DK), F32)],
        compiler_params=_params(("parallel", "parallel", "arbitrary")),
        name="gla",
    )(proj, proj, proj, proj, ag, w_alpha_up, b_alpha.reshape(1, -1), g_out.reshape(1, -1))


def _router_kernel(h_ref, g_ref, w_ref, b_ref, hn_ref, idx_ref, wt_ref, rank_ref, cnt_ref, cnt_scr):
    tm = h_ref.shape[0]

    @pl.when(pl.program_id(0) == 0)
    def _():
        cnt_scr[...] = jnp.zeros(cnt_scr.shape, F32)

    x = h_ref[...]
    ms = jnp.mean(x * x, axis=-1, keepdims=True)
    hn = x * lax.rsqrt(ms + RMS_EPS) * g_ref[...]
    hn_ref[...] = hn
    logits = jnp.dot(hn, w_ref[...], precision=lax.Precision.HIGHEST, preferred_element_type=F32) + b_ref[...]
    lane_i = lax.broadcasted_iota(jnp.int32, (tm, LANES), 1)
    lane = lane_i.astype(F32)
    logits = jnp.where(lane_i < N_EXPERTS, logits, -jnp.inf)

    row = lax.broadcasted_iota(jnp.int32, (tm, tm), 0)
    col = lax.broadcasted_iota(jnp.int32, (tm, tm), 1)
    strict = (row > col).astype(BF16)

    work = logits
    sel = jnp.zeros((tm, LANES), jnp.bool_)
    idx_out = jnp.zeros((tm, LANES), F32)
    val_out = jnp.full((tm, LANES), -jnp.inf, F32)
    onehots = []
    for kk in range(TOP_K):
        mx = jnp.max(work, axis=-1, keepdims=True)
        idx = jnp.min(jnp.where(work == mx, lane, float(LANES)), axis=-1, keepdims=True)
        onehot = lane == idx
        onehots.append(onehot)
        sel = jnp.logical_or(sel, onehot)
        idx_out = jnp.where(lane_i == kk, idx, idx_out)
        val_out = jnp.where(lane_i == kk, mx, val_out)
        work = jnp.where(onehot, -jnp.inf, work)

    e = jnp.exp(val_out - jnp.max(val_out, axis=-1, keepdims=True))
    wt_ref[...] = e / jnp.sum(e, axis=-1, keepdims=True)
    idx_ref[...] = idx_out.astype(jnp.int32)

    self = sel.astype(F32)
    rank_dense = jnp.dot(strict, self.astype(BF16), preferred_element_type=F32) + cnt_scr[...]
    rank_out = jnp.zeros((tm, LANES), F32)
    for kk in range(TOP_K):
        r = jnp.sum(jnp.where(onehots[kk], rank_dense, 0.0), axis=-1, keepdims=True)
        rank_out = jnp.where(lane_i == kk, r, rank_out)
    rank_ref[...] = rank_out.astype(jnp.int32)
    cnt_scr[...] = cnt_scr[...] + jnp.sum(self, axis=0, keepdims=True)
    cnt_ref[...] = cnt_scr[...].astype(jnp.int32)


def router(h, g, w_router, b_router, tm=256):
    t, d = h.shape
    wpad = jnp.zeros((d, LANES), F32).at[:, :N_EXPERTS].set(w_router)
    bpad = jnp.zeros((1, LANES), F32).at[0, :N_EXPERTS].set(b_router)
    tile = lambda i: (i, 0)
    const = lambda i: (0, 0)
    return pl.pallas_call(
        _router_kernel,
        out_shape=[jax.ShapeDtypeStruct((t, d), F32),
                   jax.ShapeDtypeStruct((t, LANES), jnp.int32),
                   jax.ShapeDtypeStruct((t, LANES), F32),
                   jax.ShapeDtypeStruct((t, LANES), jnp.int32),
                   jax.ShapeDtypeStruct((1, LANES), jnp.int32)],
        grid=(t // tm,),
        in_specs=[pl.BlockSpec((tm, d), tile), pl.BlockSpec((1, d), const),
                  pl.BlockSpec((d, LANES), const), pl.BlockSpec((1, LANES), const)],
        out_specs=[pl.BlockSpec((tm, d), tile), pl.BlockSpec((tm, LANES), tile),
                   pl.BlockSpec((tm, LANES), tile), pl.BlockSpec((tm, LANES), tile),
                   pl.BlockSpec((1, LANES), const)],
        scratch_shapes=[pltpu.VMEM((1, LANES), F32)],
        compiler_params=_params(("arbitrary",)),
        name="router",
    )(h, g.reshape(1, d), wpad, bpad)


def _expert_kernel(sbe_ref, sbc_ref, yst_ref, nv_ref, tok_ref, tokn_ref, hn_hbm, wgu_hbm, bgu_ref, sel_ref, wd_hbm,
                   bd_ref, y_hbm, x2d, stage, act, wgu_buf, wd_buf, wgu_bf, wd_bf, ybuf, zrows,
                   gsem, wgsem, wdsem, ysem, zsem, *, n1, n2):
    g = pl.program_id(0)
    n_sb = pl.num_programs(0)
    n_steps = n1 + n2
    nv = nv_ref[0]
    cnt = sbc_ref[g]
    cnt8 = pl.multiple_of(lax.shift_left(lax.shift_right_logical(cnt + 7, 3), 3), 8)
    e = sbe_ref[g]
    g_next = jnp.minimum(g + 1, n_sb - 1)
    has_next = g + 1 < nv
    e_next = sbe_ref[g_next]
    ystart = pl.multiple_of(yst_ref[g], 8)
    cur = lax.rem(g, 2)
    rb = MOE_RB
    big = MOE_BIG
    ch1 = MOE_CH_GU
    ch2 = MOE_CH_DN
    ha = act.shape[2]
    tn1 = wgu_buf.shape[2]
    tn2 = wd_buf.shape[2]
    sems = gsem

    def chunk_start(c):
        return jnp.where(c < n1, c * ch1, n1 * ch1 + (c - n1) * ch2)

    def issue_chunk(tref, c, size):
        start = chunk_start(c)
        slot = lax.rem(c, 2)
        for r in range(size):
            pltpu.make_async_copy(hn_hbm.at[pl.ds(tref[0, 0, start + r], 1), :],
                                  stage.at[slot, pl.ds(r, 1), :], sems.at[slot]).start()

    def wait_chunk(c, size):
        slot = lax.rem(c, 2)
        pltpu.make_async_copy(hn_hbm.at[pl.ds(0, size), :], stage.at[slot, pl.ds(0, size), :], sems.at[slot]).wait()

    def finish_chunk(c, size, xslot):
        wait_chunk(c, size)
        rows = pl.ds(pl.multiple_of(chunk_start(c), 16), size)
        x2d[xslot, rows, :] = stage[lax.rem(c, 2), :size, :].astype(BF16)

    def gu_copy(ex, j, slot):
        return pltpu.make_async_copy(wgu_hbm.at[ex, :, pl.ds(pl.multiple_of(j * tn1, tn1), tn1)],
                                     wgu_buf.at[slot], wgsem.at[slot])

    def d_copy(ex, j, slot):
        return pltpu.make_async_copy(wd_hbm.at[ex, :, pl.ds(pl.multiple_of(j * tn2, tn2), tn2)],
                                     wd_buf.at[slot], wdsem.at[slot])

    def y_piece(slot, j, r0, size):
        return pltpu.make_async_copy(
            ybuf.at[slot, pl.ds(r0, size), :],
            y_hbm.at[pl.ds(pl.multiple_of(ystart + r0, 8), size), pl.ds(pl.multiple_of(j * tn2, tn2), tn2)],
            ysem.at[slot])

    def y_issue(slot, j):
        n_a = lax.shift_right_logical(cnt8, 9)
        rem_a = cnt8 - n_a * 512
        n_b = lax.shift_right_logical(rem_a, 7)
        n_c = lax.shift_right_logical(rem_a - n_b * 128, 3)

        def a_body(i, carry):
            y_piece(slot, j, pl.multiple_of(i * 512, 512), 512).start()
            return carry

        def b_body(i, carry):
            y_piece(slot, j, pl.multiple_of(n_a * 512 + i * 128, 128), 128).start()
            return carry

        def c_body(i, carry):
            y_piece(slot, j, pl.multiple_of(n_a * 512 + n_b * 128 + i * 8, 8), 8).start()
            return carry

        lax.fori_loop(0, n_a, a_body, 0)
        lax.fori_loop(0, n_b, b_body, 0)
        lax.fori_loop(0, n_c, c_body, 0)

    def y_wait(slot):
        @pl.when(cnt8 > 0)
        def _():
            pltpu.make_async_copy(ybuf.at[slot, pl.ds(0, cnt8), :], y_hbm.at[pl.ds(0, cnt8), pl.ds(0, tn2)],
                                  ysem.at[slot]).wait()

    n_big = lax.shift_right_logical(cnt, big.bit_length() - 1)
    n_small = lax.shift_right_logical(cnt - n_big * big + (rb - 1), rb.bit_length() - 1)

    def for_row_blocks(fn):
        def big_body(i, carry):
            fn(pl.multiple_of(i * big, big), big)
            return carry

        def small_body(i, carry):
            fn(pl.multiple_of(n_big * big + i * rb, rb), rb)
            return carry

        lax.fori_loop(0, n_big, big_body, 0)
        lax.fori_loop(0, n_small, small_body, 0)

    def gu_step(s, carry):
        slot = lax.rem(s, 2)
        gu_copy(e, s, slot).wait()

        @pl.when(s + 1 < n1)
        def _():
            gu_copy(e, s + 1, 1 - slot).start()

        @pl.when(s + 1 == n1)
        def _():
            d_copy(e, 0, 0).start()

        @pl.when(s == 0)
        def _():
            finish_chunk(n_steps - 1, ch2, cur)

        @pl.when(s > 0)
        def _():
            finish_chunk(s - 1, ch1, 1 - cur)

        wgu_bf[...] = wgu_buf[slot].astype(BF16)
        issue_chunk(tokn_ref, s, ch1)
        bias = bgu_ref[0, pl.ds(s, 1), :]

        def sub(start, size):
            rows = pl.ds(start, size)
            gu = jnp.dot(x2d[cur, rows, :], wgu_bf[...], preferred_element_type=F32) + bias
            nxt = pltpu.roll(gu, tn1 - 1, 1)
            gate = jnp.minimum(gu, SWIGLU_LIMIT)
            up = jnp.clip(nxt, -SWIGLU_LIMIT, SWIGLU_LIMIT)
            a = gate * jax.nn.sigmoid(gate * SWIGLU_ALPHA) * (up + 1.0)
            act[s, rows, :] = jnp.dot(a.astype(BF16), sel_ref[...], preferred_element_type=F32).astype(BF16)

        for_row_blocks(sub)
        return carry

    def d_step(j, carry):
        slot = lax.rem(j, 2)
        d_copy(e, j, slot).wait()

        @pl.when(j + 1 < n2)
        def _():
            d_copy(e, j + 1, 1 - slot).start()

        @pl.when(jnp.logical_and(j + 1 == n2, has_next))
        def _():
            gu_copy(e_next, 0, 0).start()

        @pl.when(j == 0)
        def _():
            finish_chunk(n1 - 1, ch1, 1 - cur)

        @pl.when(j > 0)
        def _():
            finish_chunk(n1 + j - 1, ch2, 1 - cur)

        wd_bf[...] = wd_buf[slot].astype(BF16)
        issue_chunk(tokn_ref, n1 + j, ch2)
        bias = bd_ref[0, pl.ds(j, 1), :]

        @pl.when(j >= 2)
        def _():
            y_wait(slot)

        def sub(start, size):
            rows = pl.ds(start, size)
            y = bias
            for jj in range(0, n1, 2):
                a2 = jnp.concatenate([act[jj, rows, :], act[jj + 1, rows, :]], axis=1)
                y = y + jnp.dot(a2, wd_bf[jj * ha:(jj + 2) * ha, :], preferred_element_type=F32)
            ybuf[slot, rows, :] = y

        for_row_blocks(sub)
        y_issue(slot, j)
        return carry

    @pl.when(g < nv)
    def _():
        @pl.when(g == 0)
        def _():
            gu_copy(e, 0, 0).start()

            def body1(c, carry):
                issue_chunk(tok_ref, c, ch1)
                finish_chunk(c, ch1, 0)
                return carry

            def body2(c, carry):
                issue_chunk(tok_ref, c, ch2)
                finish_chunk(c, ch2, 0)
                return carry

            lax.fori_loop(0, n1, body1, 0)
            lax.fori_loop(n1, n_steps - 1, body2, 0)
            issue_chunk(tok_ref, n_steps - 1, ch2)

        lax.fori_loop(0, n1, gu_step, 0)
        lax.fori_loop(0, n2, d_step, 0)
        y_wait(0)
        y_wait(1)

        @pl.when(g == nv - 1)
        def _():
            wait_chunk(n_steps - 1, ch2)

        @pl.when(g == nv - 1)
        def _():
            zrows[...] = jnp.zeros(zrows.shape, zrows.dtype)
            first = ystart + cnt8
            n_tail = lax.shift_right_logical(y_hbm.shape[0] - first, 3)

            def tail_copy(i):
                return pltpu.make_async_copy(zrows, y_hbm.at[pl.ds(pl.multiple_of(first + i * 8, 8), 8), :], zsem)

            def start(i, carry):
                tail_copy(i).start()
                return carry

            def wait(i, carry):
                tail_copy(i).wait()
                return carry

            lax.fori_loop(0, n_tail, start, 0)
            lax.fori_loop(0, n_tail, wait, 0)


def expert_ffn(hn, tok_table, sb_expert, sb_count, sb_ystart, n_valid, n_out_rows, w_gate_up, b_gate_up, w_down,
               b_down):
    n_e, d, f2 = w_gate_up.shape
    f = f2 // 2
    r = MOE_SB
    n_sb = sb_expert.shape[0]
    n1, n2 = f2 // GU_TN, d // DN_TN
    assert n1 % 2 == 0 and n2 >= 2 and n1 * MOE_CH_GU + n2 * MOE_CH_DN == r
    ch = max(MOE_CH_GU, MOE_CH_DN)
    ha = GU_TN // 2
    sel = (jnp.arange(GU_TN)[:, None] == 2 * jnp.arange(ha)[None, :]).astype(BF16)

    def cur(g, nv):
        return jnp.maximum(jnp.minimum(g, nv[0] - 1), 0)

    def nxt(g, nv):
        return jnp.maximum(jnp.minimum(g + 1, nv[0] - 1), 0)

    const = lambda g, sbe, sbc, yst, nv: (0, 0)
    grid_spec = pltpu.PrefetchScalarGridSpec(
        num_scalar_prefetch=4,
        grid=(n_sb,),
        in_specs=[pl.BlockSpec((1, 1, r), lambda g, sbe, sbc, yst, nv: (cur(g, nv), 0, 0), memory_space=pltpu.SMEM),
                  pl.BlockSpec((1, 1, r), lambda g, sbe, sbc, yst, nv: (nxt(g, nv), 0, 0), memory_space=pltpu.SMEM),
                  pl.BlockSpec(memory_space=pl.ANY),
                  pl.BlockSpec(memory_space=pl.ANY),
                  pl.BlockSpec((1, n1, GU_TN), lambda g, sbe, sbc, yst, nv: (sbe[cur(g, nv)], 0, 0)),
                  pl.BlockSpec((GU_TN, ha), const),
                  pl.BlockSpec(memory_space=pl.ANY),
                  pl.BlockSpec((1, n2, DN_TN), lambda g, sbe, sbc, yst, nv: (sbe[cur(g, nv)], 0, 0))],
        out_specs=pl.BlockSpec(memory_space=pl.ANY),
        scratch_shapes=[pltpu.VMEM((2, r, d), BF16),
                        pltpu.VMEM((2, ch, d), F32),
                        pltpu.VMEM((n1, r, ha), BF16),
                        pltpu.VMEM((2, d, GU_TN), F32),
                        pltpu.VMEM((2, f, DN_TN), F32),
                        pltpu.VMEM((d, GU_TN), BF16),
                        pltpu.VMEM((f, DN_TN), BF16),
                        pltpu.VMEM((2, r, DN_TN), F32),
                        pltpu.VMEM((8, d), F32),
                        pltpu.SemaphoreType.DMA((2,)), pltpu.SemaphoreType.DMA((2,)),
                        pltpu.SemaphoreType.DMA((2,)), pltpu.SemaphoreType.DMA((2,)),
                        pltpu.SemaphoreType.DMA(())],
    )
    tok3 = tok_table.reshape(n_sb, 1, r)
    return pl.pallas_call(
        functools.partial(_expert_kernel, n1=n1, n2=n2),
        out_shape=jax.ShapeDtypeStruct((n_out_rows, d), F32),
        grid_spec=grid_spec,
        compiler_params=_params(("arbitrary",), vmem=MOE_VMEM_LIMIT),
        name="expert_ffn",
    )(sb_expert, sb_count, sb_ystart, n_valid, tok3, tok3, hn, w_gate_up, b_gate_up.reshape(n_e, n1, GU_TN), sel,
      w_down, b_down.reshape(n_e, n2, DN_TN))


def _combine_kernel(pos_ref, posn_ref, y_hbm, h_ref, wt_ref, g_ref, h2_ref, hn_ref, buf, sems):
    tm = h_ref.shape[0]
    i = pl.program_id(0)
    slot = lax.rem(i, 2)

    def gather(pref, sl, start):
        def body(s, carry):
            r = lax.shift_right_logical(s, TOP_K.bit_length() - 1)
            kk = lax.rem(s, TOP_K)
            cp = pltpu.make_async_copy(y_hbm.at[pl.ds(pref[0, 0, s], 1), :], buf.at[sl, kk, pl.ds(r, 1), :],
                                       sems.at[sl])
            if start:
                cp.start()
            else:
                cp.wait()
            return carry
        lax.fori_loop(0, tm * TOP_K, body, 0)

    @pl.when(i == 0)
    def _():
        gather(pos_ref, slot, True)

    @pl.when(i + 1 < pl.num_programs(0))
    def _():
        gather(posn_ref, 1 - slot, True)

    gather(pos_ref, slot, False)
    acc = h_ref[...]
    wt = wt_ref[...]
    for kk in range(TOP_K):
        acc = acc + wt[:, kk:kk + 1] * buf[slot, kk]
    h2_ref[...] = acc
    ms = jnp.mean(acc * acc, axis=-1, keepdims=True)
    hn_ref[...] = (acc * lax.rsqrt(ms + RMS_EPS) * g_ref[...]).astype(hn_ref.dtype)


def combine(y, pos_flat, h, wt, g_next, tm=128):
    t, d = h.shape
    nt = t // tm
    tile = lambda i: (i, 0)
    pos3 = pos_flat.reshape(nt, 1, tm * TOP_K)
    return pl.pallas_call(
        _combine_kernel,
        out_shape=[jax.ShapeDtypeStruct((t, d), F32), jax.ShapeDtypeStruct((t, d), BF16)],
        grid=(nt,),
        in_specs=[pl.BlockSpec((1, 1, tm * TOP_K), lambda i: (i, 0, 0), memory_space=pltpu.SMEM),
                  pl.BlockSpec((1, 1, tm * TOP_K), lambda i: (jnp.minimum(i + 1, nt - 1), 0, 0),
                               memory_space=pltpu.SMEM),
                  pl.BlockSpec(memory_space=pl.ANY),
                  pl.BlockSpec((tm, d), tile),
                  pl.BlockSpec((tm, LANES), tile),
                  pl.BlockSpec((1, d), lambda i: (0, 0))],
        out_specs=[pl.BlockSpec((tm, d), tile), pl.BlockSpec((tm, d), tile)],
        scratch_shapes=[pltpu.VMEM((2, TOP_K, tm, d), F32), pltpu.SemaphoreType.DMA((2,))],
        compiler_params=_params(("arbitrary",)),
        name="moe_combine",
    )(pos3, pos3, y, h, wt, g_next.reshape(1, d))


def _moe_layout(idx, rank, counts):
    t = idx.shape[0]
    r = MOE_SB
    n_sb = N_EXPERTS + -(-(t * TOP_K) // r)
    nsb = (counts + r - 1) // r
    sb_end = jnp.cumsum(nsb)
    sb_first = sb_end - nsb
    n_valid = sb_end[-1:]
    tok_pos = sb_first[idx] * r + rank
    tok = jnp.broadcast_to(jnp.arange(t, dtype=jnp.int32)[:, None], (t, TOP_K))
    tok_table = jnp.zeros((n_sb * r,), jnp.int32).at[tok_pos.reshape(-1)].set(tok.reshape(-1))
    rows8 = (counts + 7) // 8 * 8
    y_first = jnp.cumsum(rows8) - rows8
    pos = y_first[idx] + rank
    g = jnp.arange(n_sb, dtype=jnp.int32)
    gv = jnp.minimum(g, n_valid - 1)
    sb_expert = jnp.sum(gv[:, None] >= sb_end[None, :], axis=1).astype(jnp.int32)
    sb_local = gv - sb_first[sb_expert]
    sb_count = jnp.clip(counts[sb_expert] - sb_local * r, 0, r)
    sb_count = jnp.where(g < n_valid, sb_count, 0)
    sb_ystart = y_first[sb_expert] + sb_local * r
    n_out_rows = t * TOP_K + N_EXPERTS * 8
    i32 = lambda a: a.astype(jnp.int32)
    return i32(pos), tok_table, i32(sb_expert), i32(sb_count), i32(sb_ystart), i32(n_valid), n_out_rows


def _layer(h, p, lam_init, layer, norm_mix, w_in_all, w_alpha_up, b_alpha, lq1, lk1, lq2, lk2, g_diff_subln,
           g_gla_out, w_branch, w_merge_gate, b_merge_gate, w_out, norm_moe, w_router, b_router, w_gate_up,
           b_gate_up, w_down, b_down, norm_ple, w_ple_proj, w_ple_gate, batch, seq):
    t, d = h.shape
    xn = rmsnorm(h, norm_mix, BF16)
    gates = matmul_wcast_bias_sigmoid(xn, w_merge_gate, b_merge_gate, BF16)
    proj = matmul_wcast(xn, w_in_all, layer, BF16, n_cols=N_PROJ)
    w_ag = jnp.zeros((d, LANES), BF16).at[:, :GLA_GATE_RANK].set(w_in_all[layer, :, N_PROJ:].astype(BF16))
    ag = matmul(xn, w_ag, F32)
    w_au = jnp.zeros((LANES, GLA_KEY_WIDTH), F32).at[:GLA_GATE_RANK].set(w_alpha_up)
    o_a = diff_attention(proj, batch, seq, lq1, lk1, lq2, lk2, g_diff_subln, lam_init)
    o_b = gla(proj, ag, w_au, b_alpha, g_gla_out, batch, seq)
    merged = merge_branches(o_a, o_b, w_branch, gates)
    h = matmul_residual(merged, w_out, h)
    hn, idx, wt, rank, counts = router(h, norm_moe, w_router, b_router)
    pos, tok_table, sb_expert, sb_count, sb_ystart, n_valid, n_out_rows = _moe_layout(
        idx[:, :TOP_K], rank[:, :TOP_K], counts[0, :N_EXPERTS])
    ys = expert_ffn(hn, tok_table, sb_expert, sb_count, sb_ystart, n_valid, n_out_rows, w_gate_up, b_gate_up,
                    w_down, b_down)
    h, hn = combine(ys, pos.reshape(-1), h, wt, norm_ple)
    h = ple_update(hn, w_ple_gate, p.astype(BF16), w_ple_proj.astype(BF16), h)
    return h


def kernel(x, p, norm_mix, w_in, w_alpha_up, b_alpha, lambda_q1, lambda_k1, lambda_q2, lambda_k2, g_diff_subln, g_gla_out, w_branch, w_merge_gate, b_merge_gate, w_out, norm_moe, w_router, b_router, w_gate_up, b_gate_up, w_down, b_down, norm_ple, w_ple_proj, w_ple_gate, norm_final):
    batch, seq, d = x.shape
    depth = w_in.shape[0]
    h = x.reshape(batch * seq, d)
    for i in range(depth):
        lam_init = 0.8 - 0.6 * math.exp(-0.3 * i)
        h = _layer(h, p[i].reshape(batch * seq, -1), lam_init, i, norm_mix[i], w_in, w_alpha_up[i], b_alpha[i],
                   lambda_q1[i], lambda_k1[i], lambda_q2[i], lambda_k2[i], g_diff_subln[i], g_gla_out[i],
                   w_branch[i], w_merge_gate[i], b_merge_gate[i], w_out[i], norm_moe[i], w_router[i],
                   b_router[i], w_gate_up[i], b_gate_up[i], w_down[i], b_down[i], norm_ple[i], w_ple_proj[i],
                   w_ple_gate[i], batch, seq)
    out = rmsnorm(h, norm_final, F32)
    return out.reshape(batch, seq, d)
```

```python
import functools
import math

import jax
import jax.numpy as jnp
from jax import lax
from jax.experimental import pallas as pl
from jax.experimental.pallas import tpu as pltpu

F32 = jnp.float32
BF16 = jnp.bfloat16

RMS_EPS = 1e-6
DA_HEADS = 8
DA_HEAD_DIM = 128
DA_WIDTH = DA_HEADS * 2 * DA_HEAD_DIM
GLA_HEADS = 4
GLA_DK = 256
GLA_DV = 512
GLA_KEY_WIDTH = GLA_HEADS * GLA_DK
GLA_WIDTH = GLA_HEADS * GLA_DV
GLA_GATE_RANK = 16
GLA_TAU = 16.0
GLA_CHUNK = 64
N_PROJ = 3 * DA_WIDTH + 2 * GLA_KEY_WIDTH + 2 * GLA_WIDTH
N_EXPERTS = 32
TOP_K = 4
SWIGLU_LIMIT = 7.0
SWIGLU_ALPHA = 1.702
LANES = 128
MOE_SB = 1280
MOE_BIG = 512
MOE_RB = 128
MOE_CH_GU = 96
MOE_CH_DN = 16
MOE_PITCH = 40
GU_TN = 256
DN_TN = 512

VMEM_LIMIT = 56 * 1024 * 1024
MOE_VMEM_LIMIT = 60 * 1024 * 1024


def _params(sem, vmem=VMEM_LIMIT):
    return pltpu.CompilerParams(dimension_semantics=sem, vmem_limit_bytes=vmem)


def _rmsnorm_kernel(x_ref, g_ref, o_ref):
    x = x_ref[...]
    ms = jnp.mean(x * x, axis=-1, keepdims=True)
    o_ref[...] = (x * lax.rsqrt(ms + RMS_EPS) * g_ref[...]).astype(o_ref.dtype)


def rmsnorm(x, g, out_dtype, tm=256):
    t, d = x.shape
    return pl.pallas_call(
        _rmsnorm_kernel,
        out_shape=jax.ShapeDtypeStruct((t, d), out_dtype),
        grid=(t // tm,),
        in_specs=[pl.BlockSpec((tm, d), lambda i: (i, 0)),
                  pl.BlockSpec((1, d), lambda i: (0, 0))],
        out_specs=pl.BlockSpec((tm, d), lambda i: (i, 0)),
        compiler_params=_params(("parallel",)),
        name="rmsnorm",
    )(x, g.reshape(1, d))


def _mm_kernel(x_ref, w_ref, o_ref):
    o_ref[...] = jnp.dot(x_ref[...], w_ref[...], preferred_element_type=F32).astype(o_ref.dtype)


def matmul(x, w, out_dtype, tm=1024, tn=1024, n_cols=None):
    m, k = x.shape
    n = w.shape[1] if n_cols is None else n_cols
    tm, tn = min(tm, m), min(tn, n)
    return pl.pallas_call(
        _mm_kernel,
        out_shape=jax.ShapeDtypeStruct((m, n), out_dtype),
        grid=(m // tm, n // tn),
        in_specs=[pl.BlockSpec((tm, k), lambda i, j: (i, 0)),
                  pl.BlockSpec((k, tn), lambda i, j: (0, j))],
        out_specs=pl.BlockSpec((tm, tn), lambda i, j: (i, j)),
        compiler_params=_params(("parallel", "parallel")),
        name="matmul",
    )(x, w)


def _mm_wcast_kernel(x_ref, w_ref, o_ref, wbf):
    @pl.when(pl.program_id(1) == 0)
    def _():
        wbf[...] = w_ref[...].astype(BF16)

    o_ref[...] = jnp.dot(x_ref[...], wbf[...], preferred_element_type=F32).astype(o_ref.dtype)


def matmul_wcast(x, w, layer, out_dtype, n_cols=None, tm=1024, tn=512):
    m, k = x.shape
    n = w.shape[2] if n_cols is None else n_cols
    return pl.pallas_call(
        _mm_wcast_kernel,
        out_shape=jax.ShapeDtypeStruct((m, n), out_dtype),
        grid=(n // tn, m // tm),
        in_specs=[pl.BlockSpec((tm, k), lambda j, i: (i, 0)),
                  pl.BlockSpec((None, k, tn), lambda j, i: (layer, 0, j))],
        out_specs=pl.BlockSpec((tm, tn), lambda j, i: (i, j)),
        scratch_shapes=[pltpu.VMEM((k, tn), BF16)],
        compiler_params=_params(("arbitrary", "arbitrary")),
        name="matmul_wcast",
    )(x, w)


def _mm_wcast_bias_sigmoid_kernel(x_ref, w_ref, b_ref, o_ref, wbf):
    @pl.when(pl.program_id(1) == 0)
    def _():
        wbf[...] = w_ref[...].astype(BF16)

    z = jnp.dot(x_ref[...], wbf[...], preferred_element_type=F32) + b_ref[...]
    o_ref[...] = jax.nn.sigmoid(z).astype(o_ref.dtype)


def matmul_wcast_bias_sigmoid(x, w, b, out_dtype, tm=1024, tn=512):
    m, k = x.shape
    n = w.shape[1]
    return pl.pallas_call(
        _mm_wcast_bias_sigmoid_kernel,
        out_shape=jax.ShapeDtypeStruct((m, n), out_dtype),
        grid=(n // tn, m // tm),
        in_specs=[pl.BlockSpec((tm, k), lambda j, i: (i, 0)),
                  pl.BlockSpec((k, tn), lambda j, i: (0, j)),
                  pl.BlockSpec((1, tn), lambda j, i: (0, j))],
        out_specs=pl.BlockSpec((tm, tn), lambda j, i: (i, j)),
        scratch_shapes=[pltpu.VMEM((k, tn), BF16)],
        compiler_params=_params(("arbitrary", "arbitrary")),
        name="matmul_wcast_bias_sigmoid",
    )(x, w, b.reshape(1, n))


def _mm_residual_kernel(x_ref, w_ref, r_ref, o_ref, wbf):
    @pl.when(pl.program_id(1) == 0)
    def _():
        wbf[...] = w_ref[...].astype(BF16)

    o_ref[...] = r_ref[...] + jnp.dot(x_ref[...], wbf[...], preferred_element_type=F32)


def matmul_residual(x, w, r, tm=1024, tn=512):
    m, k = x.shape
    n = w.shape[1]
    return pl.pallas_call(
        _mm_residual_kernel,
        out_shape=jax.ShapeDtypeStruct((m, n), F32),
        grid=(n // tn, m // tm),
        in_specs=[pl.BlockSpec((tm, k), lambda j, i: (i, 0)),
                  pl.BlockSpec((k, tn), lambda j, i: (0, j)),
                  pl.BlockSpec((tm, tn), lambda j, i: (i, j))],
        out_specs=pl.BlockSpec((tm, tn), lambda j, i: (i, j)),
        scratch_shapes=[pltpu.VMEM((k, tn), BF16)],
        compiler_params=_params(("arbitrary", "arbitrary")),
        name="matmul_residual",
    )(x, w, r)


def _merge_kernel(oa_ref, ob_ref, wa_ref, wb_ref, ga_ref, gb_ref, o_ref, wa_bf, wb_bf):
    @pl.when(pl.program_id(1) == 0)
    def _():
        wa_bf[...] = wa_ref[...].astype(BF16)
        wb_bf[...] = wb_ref[...].astype(BF16)

    ya = jnp.dot(oa_ref[...], wa_bf[...], preferred_element_type=F32)
    yb = jnp.dot(ob_ref[...], wb_bf[...], preferred_element_type=F32)
    o_ref[...] = (ga_ref[...].astype(F32) * ya + gb_ref[...].astype(F32) * yb).astype(o_ref.dtype)


def merge_branches(o_a, o_b, w_branch, gates, tm=1024, tn=512):
    m, ka = o_a.shape
    kb = o_b.shape[1]
    n = w_branch.shape[1]
    nb = n // tn
    return pl.pallas_call(
        _merge_kernel,
        out_shape=jax.ShapeDtypeStruct((m, n), BF16),
        grid=(nb, m // tm),
        in_specs=[pl.BlockSpec((tm, ka), lambda j, i: (i, 0)),
                  pl.BlockSpec((tm, kb), lambda j, i: (i, 0)),
                  pl.BlockSpec((ka, tn), lambda j, i: (0, j)),
                  pl.BlockSpec((kb, tn), lambda j, i: (1, j)),
                  pl.BlockSpec((tm, tn), lambda j, i: (i, j)),
                  pl.BlockSpec((tm, tn), lambda j, i, nb=nb: (i, j + nb))],
        out_specs=pl.BlockSpec((tm, tn), lambda j, i: (i, j)),
        scratch_shapes=[pltpu.VMEM((ka, tn), BF16), pltpu.VMEM((kb, tn), BF16)],
        compiler_params=_params(("arbitrary", "arbitrary")),
        name="merge_branches",
    )(o_a, o_b, w_branch, w_branch, gates, gates)


def _ple_kernel(hn_ref, wg_ref, p_ref, wp_ref, h_ref, o_ref, wg_bf):
    @pl.when(pl.program_id(1) == 0)
    def _():
        wg_bf[...] = wg_ref[...].astype(BF16)

    g = jax.nn.sigmoid(jnp.dot(hn_ref[...], wg_bf[...], preferred_element_type=F32))
    pp = jnp.dot(p_ref[...], wp_ref[...], preferred_element_type=F32)
    o_ref[...] = h_ref[...] + g * pp


def ple_update(hn, w_gate, p, w_proj, h, tm=1024, tn=512):
    m, k = hn.shape
    kp = p.shape[1]
    n = w_gate.shape[1]
    return pl.pallas_call(
        _ple_kernel,
        out_shape=jax.ShapeDtypeStruct((m, n), F32),
        grid=(n // tn, m // tm),
        in_specs=[pl.BlockSpec((tm, k), lambda j, i: (i, 0)),
                  pl.BlockSpec((k, tn), lambda j, i: (0, j)),
                  pl.BlockSpec((tm, kp), lambda j, i: (i, 0)),
                  pl.BlockSpec((kp, tn), lambda j, i: (0, j)),
                  pl.BlockSpec((tm, tn), lambda j, i: (i, j))],
        out_specs=pl.BlockSpec((tm, tn), lambda j, i: (i, j)),
        scratch_shapes=[pltpu.VMEM((k, tn), BF16)],
        compiler_params=_params(("arbitrary", "arbitrary")),
        name="ple_update",
    )(hn, w_gate, p, w_proj, h)


ATT_BLOCK = 512
ATT_HEADS = 1

LOG2E = 1.4426950408889634


def _diff_attn_kernel(q_ref, k_ref, v_ref, lq1_ref, lk1_ref, lq2_ref, lk2_ref, g_ref, o_ref,
                      rel_scr, m_scr, l_scr, acc_scr, *, lam_init):
    tq = q_ref.shape[0]
    d = DA_HEAD_DIM
    hw = 2 * d
    hp = ATT_HEADS
    i = pl.program_id(2)
    slopes = [LOG2E * jnp.exp2((-8.0 / DA_HEADS) * (pl.program_id(1) * hp + hh + 1).astype(F32)
                               * jnp.ones((1, 1), F32)) for hh in range(hp)]

    @pl.when(i == 0)
    def _():
        row = lax.broadcasted_iota(jnp.int32, (tq, tq), 0)
        col = lax.broadcasted_iota(jnp.int32, (tq, tq), 1)
        for hh in range(hp):
            rel_scr[hh] = (row - col).astype(F32) * slopes[hh]

    q = (q_ref[...].astype(F32) * (LOG2E * d ** -0.5)).astype(BF16)

    m_scr[...] = jnp.full(m_scr.shape, -jnp.inf, F32)
    l_scr[...] = jnp.zeros(l_scr.shape, F32)
    acc_scr[...] = jnp.zeros(acc_scr.shape, F32)

    def block(j, masked):
        start = pl.multiple_of(j * tq, tq)
        state = [(m_scr[c], l_scr[c], acc_scr[c]) for c in range(2 * hp)]
        new_state = []
        for hh in range(hp):
            vj = v_ref[pl.ds(start, tq), hh * hw:(hh + 1) * hw]
            off = slopes[hh] * ((i - j) * tq).astype(F32)
            for mp in range(2):
                c = 2 * hh + mp
                kj = k_ref[pl.ds(start, tq), c * d:(c + 1) * d]
                s = lax.dot_general(q[:, c * d:(c + 1) * d], kj, (((1,), (1,)), ((), ())),
                                    preferred_element_type=F32)
                t = s - rel_scr[hh]
                if masked:
                    row = lax.broadcasted_iota(jnp.int32, (tq, tq), 0)
                    col = lax.broadcasted_iota(jnp.int32, (tq, tq), 1)
                    t = jnp.where(row >= col, t, -jnp.inf)
                m_old, l_old, acc_old = state[c]
                m_new = jnp.maximum(m_old, jnp.max(t, axis=-1, keepdims=True) - off)
                p = jnp.exp2(t - (m_new + off))
                alpha = jnp.exp2(m_old - m_new)
                l_new = alpha * l_old + jnp.sum(p, axis=-1, keepdims=True)
                acc_new = alpha * acc_old + jnp.dot(p.astype(BF16), vj, preferred_element_type=F32)
                new_state.append((m_new, l_new, acc_new))
        for c in range(2 * hp):
            m_scr[c], l_scr[c], acc_scr[c] = new_state[c]

    def body(j, carry):
        block(j, False)
        return carry

    lax.fori_loop(0, i, body, 0)
    block(i, True)

    lam = (jnp.exp(jnp.sum(lq1_ref[...] * lk1_ref[...], axis=-1, keepdims=True))
           - jnp.exp(jnp.sum(lq2_ref[...] * lk2_ref[...], axis=-1, keepdims=True)) + lam_init)
    for hh in range(hp):
        o = acc_scr[2 * hh] / l_scr[2 * hh] - lam * (acc_scr[2 * hh + 1] / l_scr[2 * hh + 1])
        ms = jnp.mean(o * o, axis=-1, keepdims=True)
        o = o * lax.rsqrt(ms + RMS_EPS) * g_ref[...] * (1.0 - lam_init)
        o_ref[:, hh * hw:(hh + 1) * hw] = o.astype(o_ref.dtype)


def diff_attention(proj, batch, seq, lq1, lk1, lq2, lk2, g_subln, lam_init):
    tq = min(ATT_BLOCK, seq)
    nq = seq // tq
    hp = ATT_HEADS
    hw = 2 * DA_HEAD_DIM
    gw = hp * hw
    kcol = DA_WIDTH // gw
    vcol = 2 * DA_WIDTH // gw
    vec = lambda a: a.reshape(1, -1).astype(F32)
    const = lambda b, h, i: (0, 0)
    return pl.pallas_call(
        functools.partial(_diff_attn_kernel, lam_init=lam_init),
        out_shape=jax.ShapeDtypeStruct((batch * seq, DA_WIDTH), BF16),
        grid=(batch, DA_HEADS // hp, nq),
        in_specs=[pl.BlockSpec((tq, gw), lambda b, h, i: (b * nq + i, h)),
                  pl.BlockSpec((seq, gw), lambda b, h, i: (b, kcol + h)),
                  pl.BlockSpec((seq, gw), lambda b, h, i: (b, vcol + h)),
                  pl.BlockSpec((1, DA_HEAD_DIM), const),
                  pl.BlockSpec((1, DA_HEAD_DIM), const),
                  pl.BlockSpec((1, DA_HEAD_DIM), const),
                  pl.BlockSpec((1, DA_HEAD_DIM), const),
                  pl.BlockSpec((1, hw), const)],
        out_specs=pl.BlockSpec((tq, gw), lambda b, h, i: (b * nq + i, h)),
        scratch_shapes=[pltpu.VMEM((hp, tq, tq), F32),
                        pltpu.VMEM((2 * hp, tq, 1), F32),
                        pltpu.VMEM((2 * hp, tq, 1), F32),
                        pltpu.VMEM((2 * hp, tq, hw), F32)],
        compiler_params=_params(("arbitrary", "arbitrary", "arbitrary")),
        name="diff_attention",
    )(proj, proj, proj, vec(lq1), vec(lk1), vec(lq2), vec(lk2), vec(g_subln))


GLA_STEP = 512


def _gla_kernel(q_ref, k_ref, v_ref, og_ref, ag_ref, wa_ref, ba_ref, g_ref, o_ref, state_scr):
    c = GLA_CHUNK
    n_chunks = q_ref.shape[0] // c

    @pl.when(pl.program_id(2) == 0)
    def _():
        state_scr[...] = jnp.zeros(state_scr.shape, F32)

    row = lax.broadcasted_iota(jnp.int32, (c, c), 0)
    col = lax.broadcasted_iota(jnp.int32, (c, c), 1)
    tril = row >= col
    tri_ones = tril.astype(F32)
    hi = lax.Precision.HIGHEST

    for n in range(n_chunks):
        sl = pl.ds(n * c, c)
        z = jnp.dot(ag_ref[sl, :], wa_ref[...], precision=hi, preferred_element_type=F32) + ba_ref[...]
        log_a = jax.nn.log_sigmoid(z) * (1.0 / GLA_TAU)
        b = jnp.dot(tri_ones, log_a, precision=hi, preferred_element_type=F32)
        b_mid = b[c // 2:c // 2 + 1, :]
        b_last = b[c - 1:c, :]
        q = q_ref[sl, :].astype(F32) * (GLA_DK ** -0.5)
        k = k_ref[sl, :].astype(F32)
        v = v_ref[sl, :]
        qg = (q * jnp.exp(b - b_mid)).astype(BF16)
        kg = (k * jnp.exp(b_mid - b)).astype(BF16)
        att = lax.dot_general(qg, kg, (((1,), (1,)), ((), ())), preferred_element_type=F32)
        att = jnp.where(tril, att, 0.0).astype(BF16)
        o = jnp.dot(att, v, preferred_element_type=F32)
        q_inter = (q * jnp.exp(b)).astype(BF16)
        state = state_scr[...]
        o = o + lax.dot_general(q_inter, state.astype(BF16), (((1,), (1,)), ((), ())),
                                preferred_element_type=F32)
        k_state = (k * jnp.exp(b_last - b)).astype(BF16)
        kv = lax.dot_general(v, k_state, (((0,), (0,)), ((), ())), preferred_element_type=F32)
        state_scr[...] = state * jnp.exp(b_last) + kv
        ms = jnp.mean(o * o, axis=-1, keepdims=True)
        og = og_ref[sl, :].astype(F32)
        o = o * lax.rsqrt(ms + RMS_EPS) * g_ref[...] * (og * jax.nn.sigmoid(og))
        o_ref[sl, :] = o.astype(o_ref.dtype)


def gla(proj, ag, w_alpha_up, b_alpha, g_out, batch, seq):
    ts = min(GLA_STEP, seq)
    ns = seq // ts
    qcol = 3 * DA_WIDTH // GLA_DK
    kcol = qcol + GLA_KEY_WIDTH // GLA_DK
    vcol = (3 * DA_WIDTH + 2 * GLA_KEY_WIDTH) // GLA_DV
    ocol = vcol + GLA_WIDTH // GLA_DV
    rank = ag.shape[1]
    return pl.pallas_call(
        _gla_kernel,
        out_shape=jax.ShapeDtypeStruct((batch * seq, GLA_WIDTH), BF16),
        grid=(batch, GLA_HEADS, ns),
        in_specs=[pl.BlockSpec((ts, GLA_DK), lambda b, h, i: (b * ns + i, qcol + h)),
                  pl.BlockSpec((ts, GLA_DK), lambda b, h, i: (b * ns + i, kcol + h)),
                  pl.BlockSpec((ts, GLA_DV), lambda b, h, i: (b * ns + i, vcol + h)),
                  pl.BlockSpec((ts, GLA_DV), lambda b, h, i: (b * ns + i, ocol + h)),
                  pl.BlockSpec((ts, rank), lambda b, h, i: (b * ns + i, 0)),
                  pl.BlockSpec((rank, GLA_DK), lambda b, h, i: (0, h)),
                  pl.BlockSpec((1, GLA_DK), lambda b, h, i: (0, h)),
                  pl.BlockSpec((1, GLA_DV), lambda b, h, i: (0, 0))],
        out_specs=pl.BlockSpec((ts, GLA_DV), lambda b, h, i: (b * ns + i, h)),
        scratch_shapes=[pltpu.VMEM((GLA_DV, GLA_DK), F32)],
        compiler_params=_params(("parallel", "parallel", "arbitrary")),
        name="gla",
    )(proj, proj, proj, proj, ag, w_alpha_up, b_alpha.reshape(1, -1), g_out.reshape(1, -1))


def _router_kernel(h_ref, g_ref, w_ref, b_ref, hn_ref, idx_ref, wt_ref, rank_ref, cnt_ref, cnt_scr):
    tm = h_ref.shape[0]

    @pl.when(pl.program_id(0) == 0)
    def _():
        cnt_scr[...] = jnp.zeros(cnt_scr.shape, F32)

    x = h_ref[...]
    ms = jnp.mean(x * x, axis=-1, keepdims=True)
    hn = x * lax.rsqrt(ms + RMS_EPS) * g_ref[...]
    hn_ref[...] = hn
    logits = jnp.dot(hn, w_ref[...], precision=lax.Precision.HIGHEST, preferred_element_type=F32) + b_ref[...]
    lane_i = lax.broadcasted_iota(jnp.int32, (tm, LANES), 1)
    lane = lane_i.astype(F32)
    logits = jnp.where(lane_i < N_EXPERTS, logits, -jnp.inf)

    row = lax.broadcasted_iota(jnp.int32, (tm, tm), 0)
    col = lax.broadcasted_iota(jnp.int32, (tm, tm), 1)
    strict = (row > col).astype(BF16)

    work = logits
    sel = jnp.zeros((tm, LANES), jnp.bool_)
    idx_out = jnp.zeros((tm, LANES), F32)
    val_out = jnp.full((tm, LANES), -jnp.inf, F32)
    onehots = []
    for kk in range(TOP_K):
        mx = jnp.max(work, axis=-1, keepdims=True)
        idx = jnp.min(jnp.where(work == mx, lane, float(LANES)), axis=-1, keepdims=True)
        onehot = lane == idx
        onehots.append(onehot)
        sel = jnp.logical_or(sel, onehot)
        idx_out = jnp.where(lane_i == kk, idx, idx_out)
        val_out = jnp.where(lane_i == kk, mx, val_out)
        work = jnp.where(onehot, -jnp.inf, work)

    e = jnp.exp(val_out - jnp.max(val_out, axis=-1, keepdims=True))
    wt_ref[...] = e / jnp.sum(e, axis=-1, keepdims=True)
    idx_ref[...] = idx_out.astype(jnp.int32)

    self = sel.astype(F32)
    rank_dense = jnp.dot(strict, self.astype(BF16), preferred_element_type=F32) + cnt_scr[...]
    rank_out = jnp.zeros((tm, LANES), F32)
    for kk in range(TOP_K):
        r = jnp.sum(jnp.where(onehots[kk], rank_dense, 0.0), axis=-1, keepdims=True)
        rank_out = jnp.where(lane_i == kk, r, rank_out)
    rank_ref[...] = rank_out.astype(jnp.int32)
    cnt_scr[...] = cnt_scr[...] + jnp.sum(self, axis=0, keepdims=True)
    cnt_ref[...] = cnt_scr[...].astype(jnp.int32)


def router(h, g, w_router, b_router, tm=256):
    t, d = h.shape
    wpad = jnp.zeros((d, LANES), F32).at[:, :N_EXPERTS].set(w_router)
    bpad = jnp.zeros((1, LANES), F32).at[0, :N_EXPERTS].set(b_router)
    tile = lambda i: (i, 0)
    const = lambda i: (0, 0)
    return pl.pallas_call(
        _router_kernel,
        out_shape=[jax.ShapeDtypeStruct((t, d), F32),
                   jax.ShapeDtypeStruct((t, LANES), jnp.int32),
                   jax.ShapeDtypeStruct((t, LANES), F32),
                   jax.ShapeDtypeStruct((t, LANES), jnp.int32),
                   jax.ShapeDtypeStruct((1, LANES), jnp.int32)],
        grid=(t // tm,),
        in_specs=[pl.BlockSpec((tm, d), tile), pl.BlockSpec((1, d), const),
                  pl.BlockSpec((d, LANES), const), pl.BlockSpec((1, LANES), const)],
        out_specs=[pl.BlockSpec((tm, d), tile), pl.BlockSpec((tm, LANES), tile),
                   pl.BlockSpec((tm, LANES), tile), pl.BlockSpec((tm, LANES), tile),
                   pl.BlockSpec((1, LANES), const)],
        scratch_shapes=[pltpu.VMEM((1, LANES), F32)],
        compiler_params=_params(("arbitrary",)),
        name="router",
    )(h, g.reshape(1, d), wpad, bpad)


def _expert_kernel(sbe_ref, sbc_ref, yst_ref, nv_ref, tok_ref, tokn_ref, hn_hbm, wgu_hbm, bgu_ref, sel_ref, wd_hbm,
                   bd_ref, y_hbm, x2d, stage, act, wgu_buf, wd_buf, wgu_bf, wd_bf, ybuf, zrows,
                   gsem, wgsem, wdsem, ysem, zsem, *, n1, n2):
    g = pl.program_id(0)
    n_sb = pl.num_programs(0)
    n_steps = n1 + n2
    nv = nv_ref[0]
    cnt = sbc_ref[g]
    cnt8 = pl.multiple_of(lax.shift_left(lax.shift_right_logical(cnt + 7, 3), 3), 8)
    e = sbe_ref[g]
    g_next = jnp.minimum(g + 1, n_sb - 1)
    has_next = g + 1 < nv
    cnt_next = jnp.where(has_next, sbc_ref[g_next], 0)
    e_next = sbe_ref[g_next]
    ystart = pl.multiple_of(yst_ref[g], 8)
    cur = lax.rem(g, 2)
    rb = MOE_RB
    big = MOE_BIG
    ch1 = MOE_CH_GU
    ch2 = MOE_CH_DN
    ha = act.shape[2]
    tn1 = wgu_buf.shape[2]
    tn2 = wd_buf.shape[2]
    sems = gsem

    sub = x2d.shape[2] // LANES
    pitch = MOE_PITCH

    def chunk_start(c):
        return jnp.where(c < n1, c * ch1, n1 * ch1 + (c - n1) * ch2)

    def chunk_count(c, size, total):
        return jnp.clip(total - chunk_start(c), 0, size)

    def issue_chunk(tref, c, size, total):
        start = chunk_start(c)
        slot = lax.rem(c, 2)

        def body(r, carry):
            tok = tref[0, 0, start + r]
            pltpu.make_async_copy(hn_hbm.at[pl.ds(pl.multiple_of(tok * sub, sub), sub), :],
                                  stage.at[slot, pl.ds(pl.multiple_of(r * pitch, 8), sub), :], sems.at[slot]).start()
            return carry
        lax.fori_loop(0, chunk_count(c, size, total), body, 0)

    def wait_chunk(c, size, total):
        slot = lax.rem(c, 2)
        n = chunk_count(c, size, total) * sub

        @pl.when(n > 0)
        def _():
            pltpu.make_async_copy(hn_hbm.at[pl.ds(0, n), :], stage.at[slot, pl.ds(0, n), :], sems.at[slot]).wait()

    def finish_chunk(c, size, total, xslot):
        wait_chunk(c, size, total)
        slot = lax.rem(c, 2)
        rows = pl.ds(pl.multiple_of(chunk_start(c), 16), size)
        for cc in range(sub):
            x2d[xslot, rows, cc * LANES:(cc + 1) * LANES] = (
                stage[slot, pl.ds(cc, size, stride=pitch), :].astype(BF16))

    def gu_copy(ex, j, slot):
        return pltpu.make_async_copy(wgu_hbm.at[ex, :, pl.ds(pl.multiple_of(j * tn1, tn1), tn1)],
                                     wgu_buf.at[slot], wgsem.at[slot])

    def d_copy(ex, j, slot):
        return pltpu.make_async_copy(wd_hbm.at[ex, :, pl.ds(pl.multiple_of(j * tn2, tn2), tn2)],
                                     wd_buf.at[slot], wdsem.at[slot])

    def y_piece(slot, j, r0, size):
        return pltpu.make_async_copy(
            ybuf.at[slot, pl.ds(r0, size), :],
            y_hbm.at[pl.ds(pl.multiple_of(ystart + r0, 8), size), pl.ds(pl.multiple_of(j * tn2, tn2), tn2)],
            ysem.at[slot])

    def y_issue(slot, j):
        n_a = lax.shift_right_logical(cnt8, 9)
        rem_a = cnt8 - n_a * 512
        n_b = lax.shift_right_logical(rem_a, 7)
        n_c = lax.shift_right_logical(rem_a - n_b * 128, 3)

        def a_body(i, carry):
            y_piece(slot, j, pl.multiple_of(i * 512, 512), 512).start()
            return carry

        def b_body(i, carry):
            y_piece(slot, j, pl.multiple_of(n_a * 512 + i * 128, 128), 128).start()
            return carry

        def c_body(i, carry):
            y_piece(slot, j, pl.multiple_of(n_a * 512 + n_b * 128 + i * 8, 8), 8).start()
            return carry

        lax.fori_loop(0, n_a, a_body, 0)
        lax.fori_loop(0, n_b, b_body, 0)
        lax.fori_loop(0, n_c, c_body, 0)

    def y_wait(slot):
        @pl.when(cnt8 > 0)
        def _():
            pltpu.make_async_copy(ybuf.at[slot, pl.ds(0, cnt8), :], y_hbm.at[pl.ds(0, cnt8), pl.ds(0, tn2)],
                                  ysem.at[slot]).wait()

    n_big = lax.shift_right_logical(cnt, big.bit_length() - 1)
    n_small = lax.shift_right_logical(cnt - n_big * big + (rb - 1), rb.bit_length() - 1)

    def for_row_blocks(fn):
        def big_body(i, carry):
            fn(pl.multiple_of(i * big, big), big)
            return carry

        def small_body(i, carry):
            fn(pl.multiple_of(n_big * big + i * rb, rb), rb)
            return carry

        lax.fori_loop(0, n_big, big_body, 0)
        lax.fori_loop(0, n_small, small_body, 0)

    def gu_step(s, carry):
        slot = lax.rem(s, 2)
        gu_copy(e, s, slot).wait()

        @pl.when(s + 1 < n1)
        def _():
            gu_copy(e, s + 1, 1 - slot).start()

        @pl.when(s + 1 == n1)
        def _():
            d_copy(e, 0, 0).start()

        @pl.when(s == 0)
        def _():
            finish_chunk(n_steps - 1, ch2, cnt, cur)

        @pl.when(s > 0)
        def _():
            finish_chunk(s - 1, ch1, cnt_next, 1 - cur)

        wgu_bf[...] = wgu_buf[slot].astype(BF16)
        issue_chunk(tokn_ref, s, ch1, cnt_next)
        bias = bgu_ref[0, pl.ds(s, 1), :]

        def sub(start, size):
            rows = pl.ds(start, size)
            gu = jnp.dot(x2d[cur, rows, :], wgu_bf[...], preferred_element_type=F32) + bias
            nxt = pltpu.roll(gu, tn1 - 1, 1)
            gate = jnp.minimum(gu, SWIGLU_LIMIT)
            up = jnp.clip(nxt, -SWIGLU_LIMIT, SWIGLU_LIMIT)
            a = gate * jax.nn.sigmoid(gate * SWIGLU_ALPHA) * (up + 1.0)
            act[s, rows, :] = jnp.dot(a.astype(BF16), sel_ref[...], preferred_element_type=F32).astype(BF16)

        for_row_blocks(sub)
        return carry

    def d_step(j, carry):
        slot = lax.rem(j, 2)
        d_copy(e, j, slot).wait()

        @pl.when(j + 1 < n2)
        def _():
            d_copy(e, j + 1, 1 - slot).start()

        @pl.when(jnp.logical_and(j + 1 == n2, has_next))
        def _():
            gu_copy(e_next, 0, 0).start()

        @pl.when(j == 0)
        def _():
            finish_chunk(n1 - 1, ch1, cnt_next, 1 - cur)

        @pl.when(j > 0)
        def _():
            finish_chunk(n1 + j - 1, ch2, cnt_next, 1 - cur)

        wd_bf[...] = wd_buf[slot].astype(BF16)
        issue_chunk(tokn_ref, n1 + j, ch2, cnt_next)
        bias = bd_ref[0, pl.ds(j, 1), :]

        @pl.when(j >= 2)
        def _():
            y_wait(slot)

        def sub(start, size):
            rows = pl.ds(start, size)
            y = bias
            for jj in range(0, n1, 2):
                a2 = jnp.concatenate([act[jj, rows, :], act[jj + 1, rows, :]], axis=1)
                y = y + jnp.dot(a2, wd_bf[jj * ha:(jj + 2) * ha, :], preferred_element_type=F32)
            ybuf[slot, rows, :] = y

        for_row_blocks(sub)
        y_issue(slot, j)
        return carry

    @pl.when(g < nv)
    def _():
        @pl.when(g == 0)
        def _():
            gu_copy(e, 0, 0).start()
            stage[...] = jnp.zeros(stage.shape, stage.dtype)

            def body1(c, carry):
                issue_chunk(tok_ref, c, ch1, cnt)
                finish_chunk(c, ch1, cnt, 0)
                return carry

            def body2(c, carry):
                issue_chunk(tok_ref, c, ch2, cnt)
                finish_chunk(c, ch2, cnt, 0)
                return carry

            lax.fori_loop(0, n1, body1, 0)
            lax.fori_loop(n1, n_steps - 1, body2, 0)
            issue_chunk(tok_ref, n_steps - 1, ch2, cnt)

        lax.fori_loop(0, n1, gu_step, 0)
        lax.fori_loop(0, n2, d_step, 0)
        y_wait(0)
        y_wait(1)

        @pl.when(g == nv - 1)
        def _():
            zrows[...] = jnp.zeros(zrows.shape, zrows.dtype)
            first = ystart + cnt8
            n_tail = lax.shift_right_logical(y_hbm.shape[0] - first, 3)

            def tail_copy(i):
                return pltpu.make_async_copy(zrows, y_hbm.at[pl.ds(pl.multiple_of(first + i * 8, 8), 8), :], zsem)

            def start(i, carry):
                tail_copy(i).start()
                return carry

            def wait(i, carry):
                tail_copy(i).wait()
                return carry

            lax.fori_loop(0, n_tail, start, 0)
            lax.fori_loop(0, n_tail, wait, 0)


def expert_ffn(hn, tok_table, sb_expert, sb_count, sb_ystart, n_valid, n_out_rows, w_gate_up, b_gate_up, w_down,
               b_down):
    n_e, d, f2 = w_gate_up.shape
    f = f2 // 2
    r = MOE_SB
    n_sb = sb_expert.shape[0]
    n1, n2 = f2 // GU_TN, d // DN_TN
    assert n1 % 2 == 0 and n2 >= 2 and n1 * MOE_CH_GU + n2 * MOE_CH_DN == r
    ch = max(MOE_CH_GU, MOE_CH_DN)
    ha = GU_TN // 2
    sel = (jnp.arange(GU_TN)[:, None] == 2 * jnp.arange(ha)[None, :]).astype(BF16)

    def cur(g, nv):
        return jnp.maximum(jnp.minimum(g, nv[0] - 1), 0)

    def nxt(g, nv):
        return jnp.maximum(jnp.minimum(g + 1, nv[0] - 1), 0)

    const = lambda g, sbe, sbc, yst, nv: (0, 0)
    grid_spec = pltpu.PrefetchScalarGridSpec(
        num_scalar_prefetch=4,
        grid=(n_sb,),
        in_specs=[pl.BlockSpec((1, 1, r), lambda g, sbe, sbc, yst, nv: (cur(g, nv), 0, 0), memory_space=pltpu.SMEM),
                  pl.BlockSpec((1, 1, r), lambda g, sbe, sbc, yst, nv: (nxt(g, nv), 0, 0), memory_space=pltpu.SMEM),
                  pl.BlockSpec(memory_space=pl.ANY),
                  pl.BlockSpec(memory_space=pl.ANY),
                  pl.BlockSpec((1, n1, GU_TN), lambda g, sbe, sbc, yst, nv: (sbe[cur(g, nv)], 0, 0)),
                  pl.BlockSpec((GU_TN, ha), const),
                  pl.BlockSpec(memory_space=pl.ANY),
                  pl.BlockSpec((1, n2, DN_TN), lambda g, sbe, sbc, yst, nv: (sbe[cur(g, nv)], 0, 0))],
        out_specs=pl.BlockSpec(memory_space=pl.ANY),
        scratch_shapes=[pltpu.VMEM((2, r, d), BF16),
                        pltpu.VMEM((2, ch * MOE_PITCH, LANES), F32),
                        pltpu.VMEM((n1, r, ha), BF16),
                        pltpu.VMEM((2, d, GU_TN), F32),
                        pltpu.VMEM((2, f, DN_TN), F32),
                        pltpu.VMEM((d, GU_TN), BF16),
                        pltpu.VMEM((f, DN_TN), BF16),
                        pltpu.VMEM((2, r, DN_TN), F32),
                        pltpu.VMEM((8, d), F32),
                        pltpu.SemaphoreType.DMA((2,)), pltpu.SemaphoreType.DMA((2,)),
                        pltpu.SemaphoreType.DMA((2,)), pltpu.SemaphoreType.DMA((2,)),
                        pltpu.SemaphoreType.DMA(())],
    )
    tok3 = tok_table.reshape(n_sb, 1, r)
    return pl.pallas_call(
        functools.partial(_expert_kernel, n1=n1, n2=n2),
        out_shape=jax.ShapeDtypeStruct((n_out_rows, d), F32),
        grid_spec=grid_spec,
        compiler_params=_params(("arbitrary",), vmem=MOE_VMEM_LIMIT),
        name="expert_ffn",
    )(sb_expert, sb_count, sb_ystart, n_valid, tok3, tok3, hn.reshape(-1, LANES), w_gate_up,
      b_gate_up.reshape(n_e, n1, GU_TN), sel,
      w_down, b_down.reshape(n_e, n2, DN_TN))


def _combine_kernel(pos_ref, posn_ref, y_hbm, h_ref, wt_ref, g_ref, h2_ref, hn_ref, buf, sems):
    tm = h_ref.shape[0]
    i = pl.program_id(0)
    slot = lax.rem(i, 2)

    def gather(pref, sl, start):
        def body(s, carry):
            r = lax.shift_right_logical(s, TOP_K.bit_length() - 1)
            kk = lax.rem(s, TOP_K)
            cp = pltpu.make_async_copy(y_hbm.at[pl.ds(pref[0, 0, s], 1), :], buf.at[sl, kk, pl.ds(r, 1), :],
                                       sems.at[sl])
            if start:
                cp.start()
            else:
                cp.wait()
            return carry
        lax.fori_loop(0, tm * TOP_K, body, 0)

    @pl.when(i == 0)
    def _():
        gather(pos_ref, slot, True)

    @pl.when(i + 1 < pl.num_programs(0))
    def _():
        gather(posn_ref, 1 - slot, True)

    gather(pos_ref, slot, False)
    acc = h_ref[...]
    wt = wt_ref[...]
    for kk in range(TOP_K):
        acc = acc + wt[:, kk:kk + 1] * buf[slot, kk]
    h2_ref[...] = acc
    ms = jnp.mean(acc * acc, axis=-1, keepdims=True)
    hn_ref[...] = (acc * lax.rsqrt(ms + RMS_EPS) * g_ref[...]).astype(hn_ref.dtype)


def combine(y, pos_flat, h, wt, g_next, tm=128):
    t, d = h.shape
    nt = t // tm
    tile = lambda i: (i, 0)
    pos3 = pos_flat.reshape(nt, 1, tm * TOP_K)
    return pl.pallas_call(
        _combine_kernel,
        out_shape=[jax.ShapeDtypeStruct((t, d), F32), jax.ShapeDtypeStruct((t, d), BF16)],
        grid=(nt,),
        in_specs=[pl.BlockSpec((1, 1, tm * TOP_K), lambda i: (i, 0, 0), memory_space=pltpu.SMEM),
                  pl.BlockSpec((1, 1, tm * TOP_K), lambda i: (jnp.minimum(i + 1, nt - 1), 0, 0),
                               memory_space=pltpu.SMEM),
                  pl.BlockSpec(memory_space=pl.ANY),
                  pl.BlockSpec((tm, d), tile),
                  pl.BlockSpec((tm, LANES), tile),
                  pl.BlockSpec((1, d), lambda i: (0, 0))],
        out_specs=[pl.BlockSpec((tm, d), tile), pl.BlockSpec((tm, d), tile)],
        scratch_shapes=[pltpu.VMEM((2, TOP_K, tm, d), F32), pltpu.SemaphoreType.DMA((2,))],
        compiler_params=_params(("arbitrary",)),
        name="moe_combine",
    )(pos3, pos3, y, h, wt, g_next.reshape(1, d))


def _moe_layout(idx, rank, counts):
    t = idx.shape[0]
    r = MOE_SB
    n_sb = N_EXPERTS + -(-(t * TOP_K) // r)
    nsb = (counts + r - 1) // r
    sb_end = jnp.cumsum(nsb)
    sb_first = sb_end - nsb
    n_valid = sb_end[-1:]
    tok_pos = sb_first[idx] * r + rank
    tok = jnp.broadcast_to(jnp.arange(t, dtype=jnp.int32)[:, None], (t, TOP_K))
    tok_table = jnp.zeros((n_sb * r,), jnp.int32).at[tok_pos.reshape(-1)].set(tok.reshape(-1))
    rows8 = (counts + 7) // 8 * 8
    y_first = jnp.cumsum(rows8) - rows8
    pos = y_first[idx] + rank
    g = jnp.arange(n_sb, dtype=jnp.int32)
    gv = jnp.minimum(g, n_valid - 1)
    sb_expert = jnp.sum(gv[:, None] >= sb_end[None, :], axis=1).astype(jnp.int32)
    sb_local = gv - sb_first[sb_expert]
    sb_count = jnp.clip(counts[sb_expert] - sb_local * r, 0, r)
    sb_count = jnp.where(g < n_valid, sb_count, 0)
    sb_ystart = y_first[sb_expert] + sb_local * r
    n_out_rows = t * TOP_K + N_EXPERTS * 8
    i32 = lambda a: a.astype(jnp.int32)
    return i32(pos), tok_table, i32(sb_expert), i32(sb_count), i32(sb_ystart), i32(n_valid), n_out_rows


def _layer(h, p, lam_init, layer, norm_mix, w_in_all, w_alpha_up, b_alpha, lq1, lk1, lq2, lk2, g_diff_subln,
           g_gla_out, w_branch, w_merge_gate, b_merge_gate, w_out, norm_moe, w_router, b_router, w_gate_up,
           b_gate_up, w_down, b_down, norm_ple, w_ple_proj, w_ple_gate, batch, seq):
    t, d = h.shape
    xn = rmsnorm(h, norm_mix, BF16)
    gates = matmul_wcast_bias_sigmoid(xn, w_merge_gate, b_merge_gate, BF16)
    proj = matmul_wcast(xn, w_in_all, layer, BF16, n_cols=N_PROJ)
    w_ag = jnp.zeros((d, LANES), BF16).at[:, :GLA_GATE_RANK].set(w_in_all[layer, :, N_PROJ:].astype(BF16))
    ag = matmul(xn, w_ag, F32)
    w_au = jnp.zeros((LANES, GLA_KEY_WIDTH), F32).at[:GLA_GATE_RANK].set(w_alpha_up)
    o_a = diff_attention(proj, batch, seq, lq1, lk1, lq2, lk2, g_diff_subln, lam_init)
    o_b = gla(proj, ag, w_au, b_alpha, g_gla_out, batch, seq)
    merged = merge_branches(o_a, o_b, w_branch, gates)
    h = matmul_residual(merged, w_out, h)
    hn, idx, wt, rank, counts = router(h, norm_moe, w_router, b_router)
    pos, tok_table, sb_expert, sb_count, sb_ystart, n_valid, n_out_rows = _moe_layout(
        idx[:, :TOP_K], rank[:, :TOP_K], counts[0, :N_EXPERTS])
    ys = expert_ffn(hn, tok_table, sb_expert, sb_count, sb_ystart, n_valid, n_out_rows, w_gate_up, b_gate_up,
                    w_down, b_down)
    h, hn = combine(ys, pos.reshape(-1), h, wt, norm_ple)
    h = ple_update(hn, w_ple_gate, p.astype(BF16), w_ple_proj.astype(BF16), h)
    return h


def kernel(x, p, norm_mix, w_in, w_alpha_up, b_alpha, lambda_q1, lambda_k1, lambda_q2, lambda_k2, g_diff_subln, g_gla_out, w_branch, w_merge_gate, b_merge_gate, w_out, norm_moe, w_router, b_router, w_gate_up, b_gate_up, w_down, b_down, norm_ple, w_ple_proj, w_ple_gate, norm_final):
    batch, seq, d = x.shape
    depth = w_in.shape[0]
    h = x.reshape(batch * seq, d)
    for i in range(depth):
        lam_init = 0.8 - 0.6 * math.exp(-0.3 * i)
        h = _layer(h, p[i].reshape(batch * seq, -1), lam_init, i, norm_mix[i], w_in, w_alpha_up[i], b_alpha[i],
                   lambda_q1[i], lambda_k1[i], lambda_q2[i], lambda_k2[i], g_diff_subln[i], g_gla_out[i],
                   w_branch[i], w_merge_gate[i], b_merge_gate[i], w_out[i], norm_moe[i], w_router[i],
                   b_router[i], w_gate_up[i], b_gate_up[i], w_down[i], b_down[i], norm_ple[i], w_ple_proj[i],
                   w_ple_gate[i], batch, seq)
    out = rmsnorm(h, norm_final, F32)
    return out.reshape(batch, seq, d)
```

```python
import functools
import math

import jax
import jax.numpy as jnp
from jax import lax
from jax.experimental import pallas as pl
from jax.experimental.pallas import tpu as pltpu

F32 = jnp.float32
BF16 = jnp.bfloat16

RMS_EPS = 1e-6
DA_HEADS = 8
DA_HEAD_DIM = 128
DA_WIDTH = DA_HEADS * 2 * DA_HEAD_DIM
GLA_HEADS = 4
GLA_DK = 256
GLA_DV = 512
GLA_KEY_WIDTH = GLA_HEADS * GLA_DK
GLA_WIDTH = GLA_HEADS * GLA_DV
GLA_GATE_RANK = 16
GLA_TAU = 16.0
GLA_CHUNK = 64
N_PROJ = 3 * DA_WIDTH + 2 * GLA_KEY_WIDTH + 2 * GLA_WIDTH
N_EXPERTS = 32
TOP_K = 4
SWIGLU_LIMIT = 7.0
SWIGLU_ALPHA = 1.702
LANES = 128
MOE_SB = 1280
MOE_BIG = 512
MOE_RB = 128
GU_TN = 256
DN_TN = 512

VMEM_LIMIT = 56 * 1024 * 1024
MOE_VMEM_LIMIT = 60 * 1024 * 1024


def _params(sem, vmem=VMEM_LIMIT):
    return pltpu.CompilerParams(dimension_semantics=sem, vmem_limit_bytes=vmem)


def _rmsnorm_kernel(x_ref, g_ref, o_ref):
    x = x_ref[...]
    ms = jnp.mean(x * x, axis=-1, keepdims=True)
    o_ref[...] = (x * lax.rsqrt(ms + RMS_EPS) * g_ref[...]).astype(o_ref.dtype)


def rmsnorm(x, g, out_dtype, tm=256):
    t, d = x.shape
    return pl.pallas_call(
        _rmsnorm_kernel,
        out_shape=jax.ShapeDtypeStruct((t, d), out_dtype),
        grid=(t // tm,),
        in_specs=[pl.BlockSpec((tm, d), lambda i: (i, 0)),
                  pl.BlockSpec((1, d), lambda i: (0, 0))],
        out_specs=pl.BlockSpec((tm, d), lambda i: (i, 0)),
        compiler_params=_params(("parallel",)),
        name="rmsnorm",
    )(x, g.reshape(1, d))


def _mm_kernel(x_ref, w_ref, o_ref):
    o_ref[...] = jnp.dot(x_ref[...], w_ref[...], preferred_element_type=F32).astype(o_ref.dtype)


def matmul(x, w, out_dtype, tm=1024, tn=1024, n_cols=None):
    m, k = x.shape
    n = w.shape[1] if n_cols is None else n_cols
    tm, tn = min(tm, m), min(tn, n)
    return pl.pallas_call(
        _mm_kernel,
        out_shape=jax.ShapeDtypeStruct((m, n), out_dtype),
        grid=(m // tm, n // tn),
        in_specs=[pl.BlockSpec((tm, k), lambda i, j: (i, 0)),
                  pl.BlockSpec((k, tn), lambda i, j: (0, j))],
        out_specs=pl.BlockSpec((tm, tn), lambda i, j: (i, j)),
        compiler_params=_params(("parallel", "parallel")),
        name="matmul",
    )(x, w)


def _mm_wcast_kernel(x_ref, w_ref, o_ref, wbf):
    @pl.when(pl.program_id(1) == 0)
    def _():
        wbf[...] = w_ref[...].astype(BF16)

    o_ref[...] = jnp.dot(x_ref[...], wbf[...], preferred_element_type=F32).astype(o_ref.dtype)


def matmul_wcast(x, w, layer, out_dtype, n_cols=None, tm=1024, tn=512):
    m, k = x.shape
    n = w.shape[2] if n_cols is None else n_cols
    return pl.pallas_call(
        _mm_wcast_kernel,
        out_shape=jax.ShapeDtypeStruct((m, n), out_dtype),
        grid=(n // tn, m // tm),
        in_specs=[pl.BlockSpec((tm, k), lambda j, i: (i, 0)),
                  pl.BlockSpec((None, k, tn), lambda j, i: (layer, 0, j))],
        out_specs=pl.BlockSpec((tm, tn), lambda j, i: (i, j)),
        scratch_shapes=[pltpu.VMEM((k, tn), BF16)],
        compiler_params=_params(("arbitrary", "arbitrary")),
        name="matmul_wcast",
    )(x, w)


def _mm_wcast_bias_sigmoid_kernel(x_ref, w_ref, b_ref, o_ref, wbf):
    @pl.when(pl.program_id(1) == 0)
    def _():
        wbf[...] = w_ref[...].astype(BF16)

    z = jnp.dot(x_ref[...], wbf[...], preferred_element_type=F32) + b_ref[...]
    o_ref[...] = jax.nn.sigmoid(z).astype(o_ref.dtype)


def matmul_wcast_bias_sigmoid(x, w, b, out_dtype, tm=1024, tn=512):
    m, k = x.shape
    n = w.shape[1]
    return pl.pallas_call(
        _mm_wcast_bias_sigmoid_kernel,
        out_shape=jax.ShapeDtypeStruct((m, n), out_dtype),
        grid=(n // tn, m // tm),
        in_specs=[pl.BlockSpec((tm, k), lambda j, i: (i, 0)),
                  pl.BlockSpec((k, tn), lambda j, i: (0, j)),
                  pl.BlockSpec((1, tn), lambda j, i: (0, j))],
        out_specs=pl.BlockSpec((tm, tn), lambda j, i: (i, j)),
        scratch_shapes=[pltpu.VMEM((k, tn), BF16)],
        compiler_params=_params(("arbitrary", "arbitrary")),
        name="matmul_wcast_bias_sigmoid",
    )(x, w, b.reshape(1, n))


def _mm_residual_kernel(x_ref, w_ref, r_ref, o_ref, wbf):
    @pl.when(pl.program_id(1) == 0)
    def _():
        wbf[...] = w_ref[...].astype(BF16)

    o_ref[...] = r_ref[...] + jnp.dot(x_ref[...], wbf[...], preferred_element_type=F32)


def matmul_residual(x, w, r, tm=1024, tn=512):
    m, k = x.shape
    n = w.shape[1]
    return pl.pallas_call(
        _mm_residual_kernel,
        out_shape=jax.ShapeDtypeStruct((m, n), F32),
        grid=(n // tn, m // tm),
        in_specs=[pl.BlockSpec((tm, k), lambda j, i: (i, 0)),
                  pl.BlockSpec((k, tn), lambda j, i: (0, j)),
                  pl.BlockSpec((tm, tn), lambda j, i: (i, j))],
        out_specs=pl.BlockSpec((tm, tn), lambda j, i: (i, j)),
        scratch_shapes=[pltpu.VMEM((k, tn), BF16)],
        compiler_params=_params(("arbitrary", "arbitrary")),
        name="matmul_residual",
    )(x, w, r)


def _merge_kernel(oa_ref, ob_ref, wa_ref, wb_ref, ga_ref, gb_ref, o_ref, wa_bf, wb_bf):
    @pl.when(pl.program_id(1) == 0)
    def _():
        wa_bf[...] = wa_ref[...].astype(BF16)
        wb_bf[...] = wb_ref[...].astype(BF16)

    ya = jnp.dot(oa_ref[...], wa_bf[...], preferred_element_type=F32)
    yb = jnp.dot(ob_ref[...], wb_bf[...], preferred_element_type=F32)
    o_ref[...] = (ga_ref[...].astype(F32) * ya + gb_ref[...].astype(F32) * yb).astype(o_ref.dtype)


def merge_branches(o_a, o_b, w_branch, gates, tm=1024, tn=512):
    m, ka = o_a.shape
    kb = o_b.shape[1]
    n = w_branch.shape[1]
    nb = n // tn
    return pl.pallas_call(
        _merge_kernel,
        out_shape=jax.ShapeDtypeStruct((m, n), BF16),
        grid=(nb, m // tm),
        in_specs=[pl.BlockSpec((tm, ka), lambda j, i: (i, 0)),
                  pl.BlockSpec((tm, kb), lambda j, i: (i, 0)),
                  pl.BlockSpec((ka, tn), lambda j, i: (0, j)),
                  pl.BlockSpec((kb, tn), lambda j, i: (1, j)),
                  pl.BlockSpec((tm, tn), lambda j, i: (i, j)),
                  pl.BlockSpec((tm, tn), lambda j, i, nb=nb: (i, j + nb))],
        out_specs=pl.BlockSpec((tm, tn), lambda j, i: (i, j)),
        scratch_shapes=[pltpu.VMEM((ka, tn), BF16), pltpu.VMEM((kb, tn), BF16)],
        compiler_params=_params(("arbitrary", "arbitrary")),
        name="merge_branches",
    )(o_a, o_b, w_branch, w_branch, gates, gates)


def _ple_kernel(hn_ref, wg_ref, p_ref, wp_ref, h_ref, o_ref, wg_bf):
    @pl.when(pl.program_id(1) == 0)
    def _():
        wg_bf[...] = wg_ref[...].astype(BF16)

    g = jax.nn.sigmoid(jnp.dot(hn_ref[...], wg_bf[...], preferred_element_type=F32))
    pp = jnp.dot(p_ref[...], wp_ref[...], preferred_element_type=F32)
    o_ref[...] = h_ref[...] + g * pp


def ple_update(hn, w_gate, p, w_proj, h, tm=1024, tn=512):
    m, k = hn.shape
    kp = p.shape[1]
    n = w_gate.shape[1]
    return pl.pallas_call(
        _ple_kernel,
        out_shape=jax.ShapeDtypeStruct((m, n), F32),
        grid=(n // tn, m // tm),
        in_specs=[pl.BlockSpec((tm, k), lambda j, i: (i, 0)),
                  pl.BlockSpec((k, tn), lambda j, i: (0, j)),
                  pl.BlockSpec((tm, kp), lambda j, i: (i, 0)),
                  pl.BlockSpec((kp, tn), lambda j, i: (0, j)),
                  pl.BlockSpec((tm, tn), lambda j, i: (i, j))],
        out_specs=pl.BlockSpec((tm, tn), lambda j, i: (i, j)),
        scratch_shapes=[pltpu.VMEM((k, tn), BF16)],
        compiler_params=_params(("arbitrary", "arbitrary")),
        name="ple_update",
    )(hn, w_gate, p, w_proj, h)


ATT_BLOCK = 512


LOG2E = 1.4426950408889634


def _diff_attn_kernel(q_ref, k_ref, v_ref, lq1_ref, lk1_ref, lq2_ref, lk2_ref, g_ref, o_ref,
                      rel_scr, m_scr, l_scr, acc_scr, *, lam_init):
    tq = q_ref.shape[0]
    d = DA_HEAD_DIM
    h = pl.program_id(1)
    i = pl.program_id(2)
    slope = LOG2E * jnp.exp2((-8.0 / DA_HEADS) * (h + 1).astype(F32) * jnp.ones((1, 1), F32))

    @pl.when(i == 0)
    def _():
        row = lax.broadcasted_iota(jnp.int32, (tq, tq), 0)
        col = lax.broadcasted_iota(jnp.int32, (tq, tq), 1)
        rel_scr[...] = (row - col).astype(F32) * slope

    q = (q_ref[...].astype(F32) * (LOG2E * d ** -0.5)).astype(BF16)
    qs = (q[:, :d], q[:, d:])

    m_scr[...] = jnp.full(m_scr.shape, -jnp.inf, F32)
    l_scr[...] = jnp.zeros(l_scr.shape, F32)
    acc_scr[...] = jnp.zeros(acc_scr.shape, F32)

    def block(j, masked):
        start = pl.multiple_of(j * tq, tq)
        vj = v_ref[pl.ds(start, tq), :]
        off = slope * ((i - j) * tq).astype(F32)
        for mp in range(2):
            kj = k_ref[pl.ds(start, tq), mp * d:(mp + 1) * d]
            s = lax.dot_general(qs[mp], kj, (((1,), (1,)), ((), ())), preferred_element_type=F32)
            t = s - rel_scr[...]
            if masked:
                row = lax.broadcasted_iota(jnp.int32, (tq, tq), 0)
                col = lax.broadcasted_iota(jnp.int32, (tq, tq), 1)
                t = jnp.where(row >= col, t, -jnp.inf)
            m_old = m_scr[mp]
            m_new = jnp.maximum(m_old, jnp.max(t, axis=-1, keepdims=True) - off)
            p = jnp.exp2(t - (m_new + off))
            alpha = jnp.exp2(m_old - m_new)
            l_scr[mp] = alpha * l_scr[mp] + jnp.sum(p, axis=-1, keepdims=True)
            acc_scr[mp] = alpha * acc_scr[mp] + jnp.dot(p.astype(BF16), vj, preferred_element_type=F32)
            m_scr[mp] = m_new

    def body(j, carry):
        block(j, False)
        return carry

    lax.fori_loop(0, i, body, 0)
    block(i, True)

    lam = (jnp.exp(jnp.sum(lq1_ref[...] * lk1_ref[...], axis=-1, keepdims=True))
           - jnp.exp(jnp.sum(lq2_ref[...] * lk2_ref[...], axis=-1, keepdims=True)) + lam_init)
    o = acc_scr[0] / l_scr[0] - lam * (acc_scr[1] / l_scr[1])
    ms = jnp.mean(o * o, axis=-1, keepdims=True)
    o = o * lax.rsqrt(ms + RMS_EPS) * g_ref[...] * (1.0 - lam_init)
    o_ref[...] = o.astype(o_ref.dtype)


def diff_attention(proj, batch, seq, lq1, lk1, lq2, lk2, g_subln, lam_init):
    tq = min(ATT_BLOCK, seq)
    nq = seq // tq
    hw = 2 * DA_HEAD_DIM
    kcol = DA_WIDTH // hw
    vcol = 2 * DA_WIDTH // hw
    vec = lambda a: a.reshape(1, -1).astype(F32)
    const = lambda b, h, i: (0, 0)
    return pl.pallas_call(
        functools.partial(_diff_attn_kernel, lam_init=lam_init),
        out_shape=jax.ShapeDtypeStruct((batch * seq, DA_WIDTH), BF16),
        grid=(batch, DA_HEADS, nq),
        in_specs=[pl.BlockSpec((tq, hw), lambda b, h, i: (b * nq + i, h)),
                  pl.BlockSpec((seq, hw), lambda b, h, i: (b, kcol + h)),
                  pl.BlockSpec((seq, hw), lambda b, h, i: (b, vcol + h)),
                  pl.BlockSpec((1, DA_HEAD_DIM), const),
                  pl.BlockSpec((1, DA_HEAD_DIM), const),
                  pl.BlockSpec((1, DA_HEAD_DIM), const),
                  pl.BlockSpec((1, DA_HEAD_DIM), const),
                  pl.BlockSpec((1, hw), const)],
        out_specs=pl.BlockSpec((tq, hw), lambda b, h, i: (b * nq + i, h)),
        scratch_shapes=[pltpu.VMEM((tq, tq), F32),
                        pltpu.VMEM((2, tq, 1), F32),
                        pltpu.VMEM((2, tq, 1), F32),
                        pltpu.VMEM((2, tq, hw), F32)],
        compiler_params=_params(("arbitrary", "arbitrary", "arbitrary")),
        name="diff_attention",
    )(proj, proj, proj, vec(lq1), vec(lk1), vec(lq2), vec(lk2), vec(g_subln))


GLA_STEP = 512


def _gla_kernel(q_ref, k_ref, v_ref, og_ref, ag_ref, wa_ref, ba_ref, g_ref, o_ref, state_scr):
    c = GLA_CHUNK
    n_chunks = q_ref.shape[0] // c

    @pl.when(pl.program_id(2) == 0)
    def _():
        state_scr[...] = jnp.zeros(state_scr.shape, F32)

    row = lax.broadcasted_iota(jnp.int32, (c, c), 0)
    col = lax.broadcasted_iota(jnp.int32, (c, c), 1)
    tril = row >= col
    tri_ones = tril.astype(F32)
    hi = lax.Precision.HIGHEST

    for n in range(n_chunks):
        sl = pl.ds(n * c, c)
        z = jnp.dot(ag_ref[sl, :], wa_ref[...], precision=hi, preferred_element_type=F32) + ba_ref[...]
        log_a = jax.nn.log_sigmoid(z) * (1.0 / GLA_TAU)
        b = jnp.dot(tri_ones, log_a, precision=hi, preferred_element_type=F32)
        b_mid = b[c // 2:c // 2 + 1, :]
        b_last = b[c - 1:c, :]
        q = q_ref[sl, :].astype(F32) * (GLA_DK ** -0.5)
        k = k_ref[sl, :].astype(F32)
        v = v_ref[sl, :]
        qg = (q * jnp.exp(b - b_mid)).astype(BF16)
        kg = (k * jnp.exp(b_mid - b)).astype(BF16)
        att = lax.dot_general(qg, kg, (((1,), (1,)), ((), ())), preferred_element_type=F32)
        att = jnp.where(tril, att, 0.0).astype(BF16)
        o = jnp.dot(att, v, preferred_element_type=F32)
        q_inter = (q * jnp.exp(b)).astype(BF16)
        state = state_scr[...]
        o = o + lax.dot_general(q_inter, state.astype(BF16), (((1,), (1,)), ((), ())),
                                preferred_element_type=F32)
        k_state = (k * jnp.exp(b_last - b)).astype(BF16)
        kv = lax.dot_general(v, k_state, (((0,), (0,)), ((), ())), preferred_element_type=F32)
        state_scr[...] = state * jnp.exp(b_last) + kv
        ms = jnp.mean(o * o, axis=-1, keepdims=True)
        og = og_ref[sl, :].astype(F32)
        o = o * lax.rsqrt(ms + RMS_EPS) * g_ref[...] * (og * jax.nn.sigmoid(og))
        o_ref[sl, :] = o.astype(o_ref.dtype)


def gla(proj, ag, w_alpha_up, b_alpha, g_out, batch, seq):
    ts = min(GLA_STEP, seq)
    ns = seq // ts
    qcol = 3 * DA_WIDTH // GLA_DK
    kcol = qcol + GLA_KEY_WIDTH // GLA_DK
    vcol = (3 * DA_WIDTH + 2 * GLA_KEY_WIDTH) // GLA_DV
    ocol = vcol + GLA_WIDTH // GLA_DV
    rank = ag.shape[1]
    return pl.pallas_call(
        _gla_kernel,
        out_shape=jax.ShapeDtypeStruct((batch * seq, GLA_WIDTH), BF16),
        grid=(batch, GLA_HEADS, ns),
        in_specs=[pl.BlockSpec((ts, GLA_DK), lambda b, h, i: (b * ns + i, qcol + h)),
                  pl.BlockSpec((ts, GLA_DK), lambda b, h, i: (b * ns + i, kcol + h)),
                  pl.BlockSpec((ts, GLA_DV), lambda b, h, i: (b * ns + i, vcol + h)),
                  pl.BlockSpec((ts, GLA_DV), lambda b, h, i: (b * ns + i, ocol + h)),
                  pl.BlockSpec((ts, rank), lambda b, h, i: (b * ns + i, 0)),
                  pl.BlockSpec((rank, GLA_DK), lambda b, h, i: (0, h)),
                  pl.BlockSpec((1, GLA_DK), lambda b, h, i: (0, h)),
                  pl.BlockSpec((1, GLA_DV), lambda b, h, i: (0, 0))],
        out_specs=pl.BlockSpec((ts, GLA_DV), lambda b, h, i: (b * ns + i, h)),
        scratch_shapes=[pltpu.VMEM((GLA_DV, GLA_DK), F32)],
        compiler_params=_params(("parallel", "parallel", "arbitrary")),
        name="gla",
    )(proj, proj, proj, proj, ag, w_alpha_up, b_alpha.reshape(1, -1), g_out.reshape(1, -1))


def _router_kernel(h_ref, g_ref, w_ref, b_ref, hn_ref, idx_ref, wt_ref, rank_ref, cnt_ref, cnt_scr):
    tm = h_ref.shape[0]

    @pl.when(pl.program_id(0) == 0)
    def _():
        cnt_scr[...] = jnp.zeros(cnt_scr.shape, F32)

    x = h_ref[...]
    ms = jnp.mean(x * x, axis=-1, keepdims=True)
    hn = x * lax.rsqrt(ms + RMS_EPS) * g_ref[...]
    hn_ref[...] = hn
    logits = jnp.dot(hn, w_ref[...], precision=lax.Precision.HIGHEST, preferred_element_type=F32) + b_ref[...]
    lane_i = lax.broadcasted_iota(jnp.int32, (tm, LANES), 1)
    lane = lane_i.astype(F32)
    logits = jnp.where(lane_i < N_EXPERTS, logits, -jnp.inf)

    row = lax.broadcasted_iota(jnp.int32, (tm, tm), 0)
    col = lax.broadcasted_iota(jnp.int32, (tm, tm), 1)
    strict = (row > col).astype(BF16)

    work = logits
    sel = jnp.zeros((tm, LANES), jnp.bool_)
    idx_out = jnp.zeros((tm, LANES), F32)
    val_out = jnp.full((tm, LANES), -jnp.inf, F32)
    onehots = []
    for kk in range(TOP_K):
        mx = jnp.max(work, axis=-1, keepdims=True)
        idx = jnp.min(jnp.where(work == mx, lane, float(LANES)), axis=-1, keepdims=True)
        onehot = lane == idx
        onehots.append(onehot)
        sel = jnp.logical_or(sel, onehot)
        idx_out = jnp.where(lane_i == kk, idx, idx_out)
        val_out = jnp.where(lane_i == kk, mx, val_out)
        work = jnp.where(onehot, -jnp.inf, work)

    e = jnp.exp(val_out - jnp.max(val_out, axis=-1, keepdims=True))
    wt_ref[...] = e / jnp.sum(e, axis=-1, keepdims=True)
    idx_ref[...] = idx_out.astype(jnp.int32)

    self = sel.astype(F32)
    rank_dense = jnp.dot(strict, self.astype(BF16), preferred_element_type=F32) + cnt_scr[...]
    rank_out = jnp.zeros((tm, LANES), F32)
    for kk in range(TOP_K):
        r = jnp.sum(jnp.where(onehots[kk], rank_dense, 0.0), axis=-1, keepdims=True)
        rank_out = jnp.where(lane_i == kk, r, rank_out)
    rank_ref[...] = rank_out.astype(jnp.int32)
    cnt_scr[...] = cnt_scr[...] + jnp.sum(self, axis=0, keepdims=True)
    cnt_ref[...] = cnt_scr[...].astype(jnp.int32)


def router(h, g, w_router, b_router, tm=256):
    t, d = h.shape
    wpad = jnp.zeros((d, LANES), F32).at[:, :N_EXPERTS].set(w_router)
    bpad = jnp.zeros((1, LANES), F32).at[0, :N_EXPERTS].set(b_router)
    tile = lambda i: (i, 0)
    const = lambda i: (0, 0)
    return pl.pallas_call(
        _router_kernel,
        out_shape=[jax.ShapeDtypeStruct((t, d), F32),
                   jax.ShapeDtypeStruct((t, LANES), jnp.int32),
                   jax.ShapeDtypeStruct((t, LANES), F32),
                   jax.ShapeDtypeStruct((t, LANES), jnp.int32),
                   jax.ShapeDtypeStruct((1, LANES), jnp.int32)],
        grid=(t // tm,),
        in_specs=[pl.BlockSpec((tm, d), tile), pl.BlockSpec((1, d), const),
                  pl.BlockSpec((d, LANES), const), pl.BlockSpec((1, LANES), const)],
        out_specs=[pl.BlockSpec((tm, d), tile), pl.BlockSpec((tm, LANES), tile),
                   pl.BlockSpec((tm, LANES), tile), pl.BlockSpec((tm, LANES), tile),
                   pl.BlockSpec((1, LANES), const)],
        scratch_shapes=[pltpu.VMEM((1, LANES), F32)],
        compiler_params=_params(("arbitrary",)),
        name="router",
    )(h, g.reshape(1, d), wpad, bpad)


def _expert_kernel(sbe_ref, sbc_ref, yst_ref, nv_ref, tok_ref, tokn_ref, hn_hbm, wgu_hbm, bgu_ref, sel_ref, wd_hbm,
                   bd_ref, y_hbm, x2d, stage, act, wgu_buf, wd_buf, wgu_bf, wd_bf, ybuf, zrows,
                   gsem, wgsem, wdsem, ysem, zsem, *, n1, n2):
    g = pl.program_id(0)
    n_sb = pl.num_programs(0)
    n_steps = n1 + n2
    nv = nv_ref[0]
    cnt = sbc_ref[g]
    cnt8 = pl.multiple_of(lax.shift_left(lax.shift_right_logical(cnt + 7, 3), 3), 8)
    e = sbe_ref[g]
    g_next = jnp.minimum(g + 1, n_sb - 1)
    has_next = g + 1 < nv
    cnt_next = jnp.where(has_next, sbc_ref[g_next], 0)
    e_next = sbe_ref[g_next]
    ystart = pl.multiple_of(yst_ref[g], 8)
    cur = lax.rem(g, 2)
    rb = MOE_RB
    big = MOE_BIG
    ch = stage.shape[1]
    ha = act.shape[2]
    tn1 = wgu_buf.shape[2]
    tn2 = wd_buf.shape[2]
    sems = gsem

    def stage_slot(c):
        return lax.rem(c, stage.shape[0])

    def chunk_rows(c, total):
        n = jnp.clip(total - c * ch, 0, ch)
        return pl.multiple_of(lax.shift_left(lax.shift_right_logical(n + 7, 3), 3), 8)

    def issue_chunk(tref, c, total):
        slot = stage_slot(c)

        def body(i, carry):
            for u in range(8):
                r = i * 8 + u
                pltpu.make_async_copy(hn_hbm.at[pl.ds(tref[0, 0, c * ch + r], 1), :],
                                      stage.at[slot, pl.ds(r, 1), :], sems.at[slot]).start()
            return carry
        lax.fori_loop(0, lax.shift_right_logical(chunk_rows(c, total), 3), body, 0)

    def finish_chunk(c, total, xslot):
        n = chunk_rows(c, total)
        slot = stage_slot(c)

        @pl.when(n > 0)
        def _():
            pltpu.make_async_copy(hn_hbm.at[pl.ds(0, n), :], stage.at[slot, pl.ds(0, n), :], sems.at[slot]).wait()

        x2d[xslot, pl.ds(pl.multiple_of(c * ch, ch), ch), :] = stage[slot].astype(BF16)

    def gather_step(st):
        @pl.when(cnt_next > 0)
        def _():
            @pl.when(st >= 2)
            def _():
                finish_chunk(st - 2, cnt_next, 1 - cur)
            issue_chunk(tokn_ref, st, cnt_next)

    def gu_copy(ex, j, slot):
        return pltpu.make_async_copy(wgu_hbm.at[ex, :, pl.ds(pl.multiple_of(j * tn1, tn1), tn1)],
                                     wgu_buf.at[slot], wgsem.at[slot])

    def d_copy(ex, j, slot):
        return pltpu.make_async_copy(wd_hbm.at[ex, :, pl.ds(pl.multiple_of(j * tn2, tn2), tn2)],
                                     wd_buf.at[slot], wdsem.at[slot])

    def y_piece(slot, j, r0, size):
        return pltpu.make_async_copy(
            ybuf.at[slot, pl.ds(r0, size), :],
            y_hbm.at[pl.ds(pl.multiple_of(ystart + r0, 8), size), pl.ds(pl.multiple_of(j * tn2, tn2), tn2)],
            ysem.at[slot])

    def y_issue(slot, j):
        n_a = lax.shift_right_logical(cnt8, 9)
        rem_a = cnt8 - n_a * 512
        n_b = lax.shift_right_logical(rem_a, 7)
        n_c = lax.shift_right_logical(rem_a - n_b * 128, 3)

        def a_body(i, carry):
            y_piece(slot, j, pl.multiple_of(i * 512, 512), 512).start()
            return carry

        def b_body(i, carry):
            y_piece(slot, j, pl.multiple_of(n_a * 512 + i * 128, 128), 128).start()
            return carry

        def c_body(i, carry):
            y_piece(slot, j, pl.multiple_of(n_a * 512 + n_b * 128 + i * 8, 8), 8).start()
            return carry

        lax.fori_loop(0, n_a, a_body, 0)
        lax.fori_loop(0, n_b, b_body, 0)
        lax.fori_loop(0, n_c, c_body, 0)

    def y_wait(slot):
        @pl.when(cnt8 > 0)
        def _():
            pltpu.make_async_copy(ybuf.at[slot, pl.ds(0, cnt8), :], y_hbm.at[pl.ds(0, cnt8), pl.ds(0, tn2)],
                                  ysem.at[slot]).wait()

    n_big = lax.shift_right_logical(cnt, big.bit_length() - 1)
    n_small = lax.shift_right_logical(cnt - n_big * big + (rb - 1), rb.bit_length() - 1)

    def for_row_blocks(fn):
        def big_body(i, carry):
            fn(pl.multiple_of(i * big, big), big)
            return carry

        def small_body(i, carry):
            fn(pl.multiple_of(n_big * big + i * rb, rb), rb)
            return carry

        lax.fori_loop(0, n_big, big_body, 0)
        lax.fori_loop(0, n_small, small_body, 0)

    def gu_step(s, carry):
        slot = lax.rem(s, 2)
        gu_copy(e, s, slot).wait()
        gather_step(s)
        wgu_bf[...] = wgu_buf[slot].astype(BF16)

        @pl.when(s + 2 < n1)
        def _():
            gu_copy(e, s + 2, slot).start()

        @pl.when(s + 2 == n1)
        def _():
            d_copy(e, 0, 0).start()

        @pl.when(s + 1 == n1)
        def _():
            d_copy(e, 1, 1).start()

        bias = bgu_ref[0, pl.ds(s, 1), :]

        def sub(start, size):
            rows = pl.ds(start, size)
            gu = jnp.dot(x2d[cur, rows, :], wgu_bf[...], preferred_element_type=F32) + bias
            nxt = pltpu.roll(gu, tn1 - 1, 1)
            gate = jnp.minimum(gu, SWIGLU_LIMIT)
            up = jnp.clip(nxt, -SWIGLU_LIMIT, SWIGLU_LIMIT)
            a = gate * jax.nn.sigmoid(gate * SWIGLU_ALPHA) * (up + 1.0)
            act[s, rows, :] = jnp.dot(a.astype(BF16), sel_ref[...], preferred_element_type=F32).astype(BF16)

        for_row_blocks(sub)
        return carry

    def d_step(j, carry):
        slot = lax.rem(j, 2)
        d_copy(e, j, slot).wait()
        gather_step(n1 + j)
        wd_bf[...] = wd_buf[slot].astype(BF16)

        @pl.when(j + 2 < n2)
        def _():
            d_copy(e, j + 2, slot).start()

        @pl.when(jnp.logical_and(j + 2 == n2, has_next))
        def _():
            gu_copy(e_next, 0, 0).start()

        @pl.when(jnp.logical_and(j + 1 == n2, has_next))
        def _():
            gu_copy(e_next, 1, 1).start()

        bias = bd_ref[0, pl.ds(j, 1), :]

        @pl.when(j >= 2)
        def _():
            y_wait(slot)

        def sub(start, size):
            rows = pl.ds(start, size)
            y = bias
            for jj in range(0, n1, 2):
                a2 = jnp.concatenate([act[jj, rows, :], act[jj + 1, rows, :]], axis=1)
                y = y + jnp.dot(a2, wd_bf[jj * ha:(jj + 2) * ha, :], preferred_element_type=F32)
            ybuf[slot, rows, :] = y

        for_row_blocks(sub)
        y_issue(slot, j)
        return carry

    @pl.when(g < nv)
    def _():
        @pl.when(g == 0)
        def _():
            gu_copy(e, 0, 0).start()
            gu_copy(e, 1, 1).start()
            stage[...] = jnp.zeros(stage.shape, stage.dtype)

            def body(c, carry):
                issue_chunk(tok_ref, c, cnt)
                finish_chunk(c, cnt, 0)
                return carry
            lax.fori_loop(0, n_steps, body, 0)

        @pl.when(g > 0)
        def _():
            finish_chunk(n_steps - 2, cnt, cur)
            finish_chunk(n_steps - 1, cnt, cur)

        lax.fori_loop(0, n1, gu_step, 0)
        lax.fori_loop(0, n2, d_step, 0)
        y_wait(0)
        y_wait(1)

        @pl.when(g == nv - 1)
        def _():
            zrows[...] = jnp.zeros(zrows.shape, zrows.dtype)
            first = ystart + cnt8
            n_tail = lax.shift_right_logical(y_hbm.shape[0] - first, 3)

            def tail_copy(i):
                return pltpu.make_async_copy(zrows, y_hbm.at[pl.ds(pl.multiple_of(first + i * 8, 8), 8), :], zsem)

            def start(i, carry):
                tail_copy(i).start()
                return carry

            def wait(i, carry):
                tail_copy(i).wait()
                return carry

            lax.fori_loop(0, n_tail, start, 0)
            lax.fori_loop(0, n_tail, wait, 0)


def expert_ffn(hn, tok_table, sb_expert, sb_count, sb_ystart, n_valid, n_out_rows, w_gate_up, b_gate_up, w_down,
               b_down):
    n_e, d, f2 = w_gate_up.shape
    f = f2 // 2
    r = MOE_SB
    n_sb = sb_expert.shape[0]
    n1, n2 = f2 // GU_TN, d // DN_TN
    assert n1 % 2 == 0 and n1 >= 2 and n2 >= 2 and r % (n1 + n2) == 0
    ch = r // (n1 + n2)
    ha = GU_TN // 2
    sel = (jnp.arange(GU_TN)[:, None] == 2 * jnp.arange(ha)[None, :]).astype(BF16)

    def cur(g, nv):
        return jnp.maximum(jnp.minimum(g, nv[0] - 1), 0)

    def nxt(g, nv):
        return jnp.maximum(jnp.minimum(g + 1, nv[0] - 1), 0)

    const = lambda g, sbe, sbc, yst, nv: (0, 0)
    grid_spec = pltpu.PrefetchScalarGridSpec(
        num_scalar_prefetch=4,
        grid=(n_sb,),
        in_specs=[pl.BlockSpec((1, 1, r), lambda g, sbe, sbc, yst, nv: (cur(g, nv), 0, 0), memory_space=pltpu.SMEM),
                  pl.BlockSpec((1, 1, r), lambda g, sbe, sbc, yst, nv: (nxt(g, nv), 0, 0), memory_space=pltpu.SMEM),
                  pl.BlockSpec(memory_space=pl.ANY),
                  pl.BlockSpec(memory_space=pl.ANY),
                  pl.BlockSpec((1, n1, GU_TN), lambda g, sbe, sbc, yst, nv: (sbe[cur(g, nv)], 0, 0)),
                  pl.BlockSpec((GU_TN, ha), const),
                  pl.BlockSpec(memory_space=pl.ANY),
                  pl.BlockSpec((1, n2, DN_TN), lambda g, sbe, sbc, yst, nv: (sbe[cur(g, nv)], 0, 0))],
        out_specs=pl.BlockSpec(memory_space=pl.ANY),
        scratch_shapes=[pltpu.VMEM((2, r, d), BF16),
                        pltpu.VMEM((3, ch, d), F32),
                        pltpu.VMEM((n1, r, ha), BF16),
                        pltpu.VMEM((2, d, GU_TN), F32),
                        pltpu.VMEM((2, f, DN_TN), F32),
                        pltpu.VMEM((d, GU_TN), BF16),
                        pltpu.VMEM((f, DN_TN), BF16),
                        pltpu.VMEM((2, r, DN_TN), F32),
                        pltpu.VMEM((8, d), F32),
                        pltpu.SemaphoreType.DMA((3,)), pltpu.SemaphoreType.DMA((2,)),
                        pltpu.SemaphoreType.DMA((2,)), pltpu.SemaphoreType.DMA((2,)),
                        pltpu.SemaphoreType.DMA(())],
    )
    tok3 = tok_table.reshape(n_sb, 1, r)
    return pl.pallas_call(
        functools.partial(_expert_kernel, n1=n1, n2=n2),
        out_shape=jax.ShapeDtypeStruct((n_out_rows, d), F32),
        grid_spec=grid_spec,
        compiler_params=_params(("arbitrary",), vmem=MOE_VMEM_LIMIT),
        name="expert_ffn",
    )(sb_expert, sb_count, sb_ystart, n_valid, tok3, tok3, hn, w_gate_up, b_gate_up.reshape(n_e, n1, GU_TN), sel,
      w_down, b_down.reshape(n_e, n2, DN_TN))


def _combine_kernel(pos_ref, posn_ref, y_hbm, h_ref, wt_ref, g_ref, h2_ref, hn_ref, buf, sems):
    tm = h_ref.shape[0]
    i = pl.program_id(0)
    slot = lax.rem(i, 2)

    def gather(pref, sl, start):
        def body(s, carry):
            r = lax.shift_right_logical(s, TOP_K.bit_length() - 1)
            kk = lax.rem(s, TOP_K)
            cp = pltpu.make_async_copy(y_hbm.at[pl.ds(pref[0, 0, s], 1), :], buf.at[sl, kk, pl.ds(r, 1), :],
                                       sems.at[sl])
            if start:
                cp.start()
            else:
                cp.wait()
            return carry
        lax.fori_loop(0, tm * TOP_K, body, 0)

    @pl.when(i == 0)
    def _():
        gather(pos_ref, slot, True)

    @pl.when(i + 1 < pl.num_programs(0))
    def _():
        gather(posn_ref, 1 - slot, True)

    gather(pos_ref, slot, False)
    acc = h_ref[...]
    wt = wt_ref[...]
    for kk in range(TOP_K):
        acc = acc + wt[:, kk:kk + 1] * buf[slot, kk]
    h2_ref[...] = acc
    ms = jnp.mean(acc * acc, axis=-1, keepdims=True)
    hn_ref[...] = (acc * lax.rsqrt(ms + RMS_EPS) * g_ref[...]).astype(hn_ref.dtype)


def combine(y, pos_flat, h, wt, g_next, tm=128):
    t, d = h.shape
    nt = t // tm
    tile = lambda i: (i, 0)
    pos3 = pos_flat.reshape(nt, 1, tm * TOP_K)
    return pl.pallas_call(
        _combine_kernel,
        out_shape=[jax.ShapeDtypeStruct((t, d), F32), jax.ShapeDtypeStruct((t, d), BF16)],
        grid=(nt,),
        in_specs=[pl.BlockSpec((1, 1, tm * TOP_K), lambda i: (i, 0, 0), memory_space=pltpu.SMEM),
                  pl.BlockSpec((1, 1, tm * TOP_K), lambda i: (jnp.minimum(i + 1, nt - 1), 0, 0),
                               memory_space=pltpu.SMEM),
                  pl.BlockSpec(memory_space=pl.ANY),
                  pl.BlockSpec((tm, d), tile),
                  pl.BlockSpec((tm, LANES), tile),
                  pl.BlockSpec((1, d), lambda i: (0, 0))],
        out_specs=[pl.BlockSpec((tm, d), tile), pl.BlockSpec((tm, d), tile)],
        scratch_shapes=[pltpu.VMEM((2, TOP_K, tm, d), F32), pltpu.SemaphoreType.DMA((2,))],
        compiler_params=_params(("arbitrary",)),
        name="moe_combine",
    )(pos3, pos3, y, h, wt, g_next.reshape(1, d))


def _moe_layout(idx, rank, counts):
    t = idx.shape[0]
    r = MOE_SB
    n_sb = N_EXPERTS + -(-(t * TOP_K) // r)
    nsb = (counts + r - 1) // r
    sb_end = jnp.cumsum(nsb)
    sb_first = sb_end - nsb
    n_valid = sb_end[-1:]
    tok_pos = sb_first[idx] * r + rank
    tok = jnp.broadcast_to(jnp.arange(t, dtype=jnp.int32)[:, None], (t, TOP_K))
    tok_table = jnp.zeros((n_sb * r,), jnp.int32).at[tok_pos.reshape(-1)].set(tok.reshape(-1))
    rows8 = (counts + 7) // 8 * 8
    y_first = jnp.cumsum(rows8) - rows8
    pos = y_first[idx] + rank
    g = jnp.arange(n_sb, dtype=jnp.int32)
    gv = jnp.minimum(g, n_valid - 1)
    sb_expert = jnp.sum(gv[:, None] >= sb_end[None, :], axis=1).astype(jnp.int32)
    sb_local = gv - sb_first[sb_expert]
    sb_count = jnp.clip(counts[sb_expert] - sb_local * r, 0, r)
    sb_count = jnp.where(g < n_valid, sb_count, 0)
    sb_ystart = y_first[sb_expert] + sb_local * r
    n_out_rows = t * TOP_K + N_EXPERTS * 8
    i32 = lambda a: a.astype(jnp.int32)
    return i32(pos), tok_table, i32(sb_expert), i32(sb_count), i32(sb_ystart), i32(n_valid), n_out_rows


def _layer(h, p, lam_init, layer, norm_mix, w_in_all, w_alpha_up, b_alpha, lq1, lk1, lq2, lk2, g_diff_subln,
           g_gla_out, w_branch, w_merge_gate, b_merge_gate, w_out, norm_moe, w_router, b_router, w_gate_up,
           b_gate_up, w_down, b_down, norm_ple, w_ple_proj, w_ple_gate, batch, seq):
    t, d = h.shape
    xn = rmsnorm(h, norm_mix, BF16)
    proj = matmul_wcast(xn, w_in_all, layer, BF16, n_cols=N_PROJ)
    w_ag = jnp.zeros((d, LANES), BF16).at[:, :GLA_GATE_RANK].set(w_in_all[layer, :, N_PROJ:].astype(BF16))
    ag = matmul(xn, w_ag, F32)
    w_au = jnp.zeros((LANES, GLA_KEY_WIDTH), F32).at[:GLA_GATE_RANK].set(w_alpha_up)
    gates = matmul_wcast_bias_sigmoid(xn, w_merge_gate, b_merge_gate, BF16)
    o_a = diff_attention(proj, batch, seq, lq1, lk1, lq2, lk2, g_diff_subln, lam_init)
    o_b = gla(proj, ag, w_au, b_alpha, g_gla_out, batch, seq)
    merged = merge_branches(o_a, o_b, w_branch, gates)
    h = matmul_residual(merged, w_out, h)
    hn, idx, wt, rank, counts = router(h, norm_moe, w_router, b_router)
    pos, tok_table, sb_expert, sb_count, sb_ystart, n_valid, n_out_rows = _moe_layout(
        idx[:, :TOP_K], rank[:, :TOP_K], counts[0, :N_EXPERTS])
    ys = expert_ffn(hn, tok_table, sb_expert, sb_count, sb_ystart, n_valid, n_out_rows, w_gate_up, b_gate_up,
                    w_down, b_down)
    h, hn = combine(ys, pos.reshape(-1), h, wt, norm_ple)
    h = ple_update(hn, w_ple_gate, p.astype(BF16), w_ple_proj.astype(BF16), h)
    return h


def kernel(x, p, norm_mix, w_in, w_alpha_up, b_alpha, lambda_q1, lambda_k1, lambda_q2, lambda_k2, g_diff_subln, g_gla_out, w_branch, w_merge_gate, b_merge_gate, w_out, norm_moe, w_router, b_router, w_gate_up, b_gate_up, w_down, b_down, norm_ple, w_ple_proj, w_ple_gate, norm_final):
    batch, seq, d = x.shape
    depth = w_in.shape[0]
    h = x.reshape(batch * seq, d)
    for i in range(depth):
        lam_init = 0.8 - 0.6 * math.exp(-0.3 * i)
        h = _layer(h, p[i].reshape(batch * seq, -1), lam_init, i, norm_mix[i], w_in, w_alpha_up[i], b_alpha[i],
                   lambda_q1[i], lambda_k1[i], lambda_q2[i], lambda_k2[i], g_diff_subln[i], g_gla_out[i],
                   w_branch[i], w_merge_gate[i], b_merge_gate[i], w_out[i], norm_moe[i], w_router[i],
                   b_router[i], w_gate_up[i], b_gate_up[i], w_down[i], b_down[i], norm_ple[i], w_ple_proj[i],
                   w_ple_gate[i], batch, seq)
    out = rmsnorm(h, norm_final, F32)
    return out.reshape(batch, seq, d)
```
